```python
import jax, jax.numpy as jnp
from jax import lax
import numpy as np

D_MODEL = 2048
BATCH = 4
SEQ = 4096
DEPTH = 2

GRID_W = 64
CTX_LEN = 256
HEAD_DIM = 128
ROPE_THETA = 10000.0
EPS = 1e-6
NEG = -1e30

NA_HEADS = 8
NA_WIN_ROWS = 8
NA_WIN_COLS = 16
NA_COL_QBLOCK = 16
NA_COL_KSPAN = 32
WG_HEADS = 8
WG_KV_HEADS = 2
WG_WINDOW = 128
WG_BLOCK = 128
EVEN_IN = (3 * NA_HEADS + WG_HEADS + 2 * WG_KV_HEADS) * HEAD_DIM
EVEN_MIX = (NA_HEADS + WG_HEADS) * HEAD_DIM
MLA_HEADS = 16
MLA_Q_RANK = 512
MLA_KV_RANK = 512
MLA_NOPE = 128
MLA_ROPE = 64
MLA_V = 128
MLA_QBLOCK = 128
ODD_IN = MLA_Q_RANK + MLA_KV_RANK + MLA_ROPE
ODD_MIX = MLA_HEADS * MLA_V
N_EXPERTS = 16
N_GROUPS = 4
EXPERTS_PER_GROUP = N_EXPERTS // N_GROUPS
TOPK_GROUPS = 1
TOP_K = 2
D_EXPERT = 512
D_SHARED = 1024

kernel_name = 'hybrid_natten_swa_mla_moe_dit'

F32 = jnp.float32


def rms_norm(x, g):
    xf = x.astype(F32)
    y = xf * lax.rsqrt(jnp.mean(xf * xf, axis=-1, keepdims=True) + EPS)
    return (y * g.astype(F32)).astype(x.dtype)


def axial_rope_tables(n_tok, rot_dim):
    t = jnp.arange(n_tok, dtype=jnp.int32)
    row = (t // GRID_W).astype(F32)
    col = (t % GRID_W).astype(F32)
    quarter = rot_dim // 4
    inv_freq = ROPE_THETA ** (-jnp.arange(quarter, dtype=F32) / quarter)
    ang_r = row[:, None] * inv_freq[None, :]
    ang_c = col[:, None] * inv_freq[None, :]
    ang = jnp.concatenate([ang_r, ang_r, ang_c, ang_c], axis=-1)
    return jnp.cos(ang), jnp.sin(ang)


def apply_axial_rope(x, cos, sin):
    q = x.shape[-1] // 4
    x1, x2, x3, x4 = x[..., :q], x[..., q:2 * q], x[..., 2 * q:3 * q], x[..., 3 * q:]
    rot = jnp.concatenate([-x2, x1, -x4, x3], axis=-1)
    out = x.astype(F32) * cos[None, :, None, :] + rot.astype(F32) * sin[None, :, None, :]
    return out.astype(x.dtype)


def dense_attention(q, k, v, sink):
    bsz, n_qt, n_q, dq = q.shape
    n_kv = k.shape[2]
    grp = n_q // n_kv
    n_key = k.shape[1]
    qg = q.reshape(bsz, n_qt, n_kv, grp, dq)
    logits = jnp.einsum('bqkgd,bmkd->bkgqm', qg, k).astype(F32) * dq ** -0.5
    if sink is not None:
        sink_l = jnp.broadcast_to(sink.astype(F32).reshape(1, n_kv, grp, 1, 1), logits.shape[:4] + (1,))
        logits = jnp.concatenate([logits, sink_l], axis=-1)
    p = jax.nn.softmax(logits, axis=-1)[..., :n_key].astype(v.dtype)
    out = jnp.einsum('bkgqm,bmkd->bqkgd', p, v)
    return out.reshape(bsz, n_qt, n_q, v.shape[-1])


def neighbourhood_attention(q, k, v, k_ctx, v_ctx, rpb):
    bsz, n_tok, n_heads, dh = q.shape
    rows = n_tok // GRID_W
    wr = min(NA_WIN_ROWS, rows)
    qbr = next(b for b in (8, 4, 2, 1) if b <= wr and rows % b == 0)
    ksr = min(wr + qbr, rows)
    n_rb = rows // qbr
    n_cb = GRID_W // NA_COL_QBLOCK
    q_rows = jnp.arange(n_rb)[:, None] * qbr + jnp.arange(qbr)[None, :]
    q_cols = jnp.arange(n_cb)[:, None] * NA_COL_QBLOCK + jnp.arange(NA_COL_QBLOCK)[None, :]
    k_rows = jnp.clip(q_rows[:, :1] - wr // 2, 0, rows - ksr) + jnp.arange(ksr)[None, :]
    k_cols = jnp.clip(q_cols[:, :1] - NA_WIN_COLS // 2, 0, GRID_W - NA_COL_KSPAN) + jnp.arange(NA_COL_KSPAN)[None, :]
    r_start = jnp.clip(q_rows - wr // 2, 0, rows - wr)
    c_start = jnp.clip(q_cols - NA_WIN_COLS // 2, 0, GRID_W - NA_WIN_COLS)
    kr = k_rows[:, None, :]
    kc = k_cols[:, None, :]
    row_ok = (kr >= r_start[:, :, None]) & (kr < r_start[:, :, None] + wr)
    col_ok = (kc >= c_start[:, :, None]) & (kc < c_start[:, :, None] + NA_WIN_COLS)
    mask = row_ok[:, None, :, None, :, None] & col_ok[None, :, None, :, None, :]
    dr = jnp.clip(kr - q_rows[:, :, None] + NA_WIN_ROWS - 1, 0, 2 * NA_WIN_ROWS - 2)
    dc = jnp.clip(kc - q_cols[:, :, None] + NA_WIN_COLS - 1, 0, 2 * NA_WIN_COLS - 2)
    bias = rpb.astype(F32)[:, dr[:, None, :, None, :, None], dc[None, :, None, :, None, :]]
    qg = q.reshape(bsz, n_rb, qbr, n_cb, NA_COL_QBLOCK, n_heads, dh)
    kgrid = k.reshape(bsz, rows, GRID_W, n_heads, dh)
    vgrid = v.reshape(bsz, rows, GRID_W, n_heads, dh)
    ri = k_rows[:, None, :, None]
    ci = k_cols[None, :, None, :]
    kg = kgrid[:, ri, ci]
    vg = vgrid[:, ri, ci]
    scale = dh ** -0.5
    loc = jnp.einsum('bipjqhd,bijkmhd->bhijpqkm', qg, kg).astype(F32) * scale + bias[None]
    loc = jnp.where(mask[None, None], loc, NEG)
    ctx_l = jnp.einsum('bipjqhd,bchd->bhijpqc', qg, k_ctx).astype(F32) * scale
    n_loc = ksr * NA_COL_KSPAN
    p = jax.nn.softmax(jnp.concatenate([loc.reshape(loc.shape[:6] + (n_loc,)), ctx_l], axis=-1), axis=-1)
    p = p.astype(v.dtype)
    p_loc = p[..., :n_loc].reshape(loc.shape)
    p_ctx = p[..., n_loc:]
    out = jnp.einsum('bhijpqkm,bijkmhd->bipjqhd', p_loc, vg) + jnp.einsum('bhijpqc,bchd->bipjqhd', p_ctx, v_ctx)
    return out.reshape(bsz, n_tok, n_heads, dh)


def window_gqa_attention(q, k, v, k_ctx, v_ctx, sink):
    bsz, n_tok, n_q, dh = q.shape
    n_kv = k.shape[2]
    grp = n_q // n_kv
    n_blk = n_tok // WG_BLOCK
    span = WG_BLOCK + 2 * WG_WINDOW
    pad = ((0, 0), (WG_WINDOW, WG_WINDOW), (0, 0), (0, 0))
    idx = jnp.arange(n_blk)[:, None] * WG_BLOCK + jnp.arange(span)[None, :]
    kb = jnp.pad(k, pad)[:, idx]
    vb = jnp.pad(v, pad)[:, idx]
    qb = q.reshape(bsz, n_blk, WG_BLOCK, n_kv, grp, dh)
    k_pos = idx - WG_WINDOW
    q_pos = jnp.arange(n_blk)[:, None] * WG_BLOCK + jnp.arange(WG_BLOCK)[None, :]
    rel = k_pos[:, None, :] - q_pos[:, :, None]
    mask = (k_pos[:, None, :] >= 0) & (k_pos[:, None, :] < n_tok) & (jnp.abs(rel) <= WG_WINDOW)
    scale = dh ** -0.5
    loc = jnp.einsum('bnqkgd,bnmkd->bkgnqm', qb, kb).astype(F32) * scale
    loc = jnp.where(mask, loc, NEG)
    ctx_l = jnp.einsum('bnqkgd,bckd->bkgnqc', qb, k_ctx).astype(F32) * scale
    sink_l = jnp.broadcast_to(sink.astype(F32).reshape(1, n_kv, grp, 1, 1, 1), loc.shape[:5] + (1,))
    p = jax.nn.softmax(jnp.concatenate([loc, ctx_l, sink_l], axis=-1), axis=-1).astype(v.dtype)
    p_loc = p[..., :span]
    p_ctx = p[..., span:span + k_ctx.shape[1]]
    out = jnp.einsum('bkgnqm,bnmkd->bnqkgd', p_loc, vb) + jnp.einsum('bkgnqc,bckd->bnqkgd', p_ctx, v_ctx)
    return out.reshape(bsz, n_tok, n_q, dh)


def even_mixer(hx, hc, w_in, w_out, rpb, sink, cos, sin, with_ctx):
    bsz, n_tok, _ = hx.shape
    n_ctx = hc.shape[1]
    widths = [NA_HEADS * HEAD_DIM] * 3 + [WG_HEADS * HEAD_DIM, WG_KV_HEADS * HEAD_DIM, WG_KV_HEADS * HEAD_DIM]
    heads = (NA_HEADS, NA_HEADS, NA_HEADS, WG_HEADS, WG_KV_HEADS, WG_KV_HEADS)
    cuts = [sum(widths[:i + 1]) for i in range(len(widths) - 1)]

    def split_heads(z):
        parts = jnp.split(z, cuts, axis=-1)
        return [p.reshape(p.shape[:2] + (h, HEAD_DIM)) for p, h in zip(parts, heads)]

    qa, ka, va, qw, kw, vw = split_heads(hx @ w_in)
    qac, kac, vac, qwc, kwc, vwc = split_heads(hc @ w_in)
    qw = apply_axial_rope(qw, cos, sin)
    kw = apply_axial_rope(kw, cos, sin)
    ya = neighbourhood_attention(qa, ka, va, kac, vac, rpb)
    yw = window_gqa_attention(qw, kw, vw, kwc, vwc, sink)
    yx = jnp.concatenate([ya, yw], axis=2).reshape(bsz, n_tok, EVEN_MIX) @ w_out
    yc = None
    if with_ctx:
        yac = dense_attention(qac, kac, vac, None)
        ywc = dense_attention(qwc, kwc, vwc, sink)
        yc = jnp.concatenate([yac, ywc], axis=2).reshape(bsz, n_ctx, EVEN_MIX) @ w_out
    return yx, yc


def mla_qkv(h, w_in, q_norm, kv_norm, w_qb, w_kvb, rope, need_q):
    bsz, n, _ = h.shape
    cq, ckv, k_rope = jnp.split(h @ w_in, [MLA_Q_RANK, MLA_Q_RANK + MLA_KV_RANK], axis=-1)
    kv = (rms_norm(ckv, kv_norm) @ w_kvb).reshape(bsz, n, MLA_HEADS, MLA_NOPE + MLA_V)
    k_nope, v = jnp.split(kv, [MLA_NOPE], axis=-1)
    k_rope = k_rope[:, :, None, :]
    q = None
    if need_q:
        q = (rms_norm(cq, q_norm) @ w_qb).reshape(bsz, n, MLA_HEADS, MLA_NOPE + MLA_ROPE)
        q_nope, q_rope = jnp.split(q, [MLA_NOPE], axis=-1)
        if rope is not None:
            q_rope = apply_axial_rope(q_rope, rope[0], rope[1])
        q = jnp.concatenate([q_nope, q_rope], axis=-1)
    if rope is not None:
        k_rope = apply_axial_rope(k_rope, rope[0], rope[1])
    k = jnp.concatenate([k_nope, jnp.broadcast_to(k_rope, (bsz, n, MLA_HEADS, MLA_ROPE))], axis=-1)
    return q, k, v


def odd_mixer(hx, hc, w_in, q_norm, kv_norm, w_qb, w_kvb, w_out, cos, sin, with_ctx):
    bsz, n_tok, _ = hx.shape
    qx, kx, vx = mla_qkv(hx, w_in, q_norm, kv_norm, w_qb, w_kvb, (cos, sin), True)
    qc, kc, vc = mla_qkv(hc, w_in, q_norm, kv_norm, w_qb, w_kvb, None, with_ctx)
    k_all = jnp.concatenate([kx, kc], axis=1)
    v_all = jnp.concatenate([vx, vc], axis=1)
    n_blk = n_tok // MLA_QBLOCK
    q_blocks = jnp.moveaxis(qx.reshape(bsz, n_blk, MLA_QBLOCK, MLA_HEADS, MLA_NOPE + MLA_ROPE), 1, 0)
    y = lax.map(lambda qb: dense_attention(qb, k_all, v_all, None), q_blocks)
    yx = jnp.moveaxis(y, 0, 1).reshape(bsz, n_tok, ODD_MIX) @ w_out
    yc = None
    if with_ctx:
        yc = dense_attention(qc, kc, vc, None).reshape(bsz, hc.shape[1], ODD_MIX) @ w_out
    return yx, yc


def moe_ffn(h, w_router, b_router, w_gate, w_up, w_down, ws_gate, ws_up, ws_down):
    shape = h.shape
    t = h.reshape(-1, shape[-1])
    n = t.shape[0]
    scores = jax.nn.sigmoid((t @ w_router).astype(F32))
    sel = scores + b_router.astype(F32)
    sel_g = sel.reshape(n, N_GROUPS, EXPERTS_PER_GROUP)
    group_score = jnp.sum(lax.top_k(sel_g, TOP_K)[0], axis=-1)
    _, g_idx = lax.top_k(group_score, TOPK_GROUPS)
    g_keep = jnp.any(g_idx[:, :, None] == jnp.arange(N_GROUPS)[None, None, :], axis=1)
    sel_masked = jnp.where(g_keep[:, :, None], sel_g, NEG).reshape(n, N_EXPERTS)
    _, e_idx = lax.top_k(sel_masked, TOP_K)
    w = jnp.take_along_axis(scores, e_idx, axis=-1)
    w = w / jnp.sum(w, axis=-1, keepdims=True)
    gates = jnp.sum(jax.nn.one_hot(e_idx, N_EXPERTS, dtype=F32) * w[..., None], axis=1).astype(t.dtype)
    a = jnp.einsum('nd,edf->nef', t, w_gate)
    u = jnp.einsum('nd,edf->nef', t, w_up)
    routed = jnp.einsum('nef,efd->nd', jax.nn.silu(a) * u * gates[:, :, None], w_down)
    shared = (jax.nn.silu(t @ ws_gate) * (t @ ws_up)) @ ws_down
    return (routed + shared).reshape(shape)


def setup_inputs(seed: int = 0) -> dict:
    key = jax.random.key(seed)
    keys = iter(list(jax.random.split(key, 32)))

    def nrm(shape, std):
        return jax.random.normal(next(keys), shape, F32) * std

    def lin(shape, fan_in, gain=1.0):
        return nrm(shape, gain * fan_in ** -0.5)

    def norm_gain(shape):
        return 1.0 + nrm(shape, 0.05)

    n_even = (DEPTH + 1) // 2
    n_odd = DEPTH // 2
    return {
        'x': nrm((BATCH, SEQ, D_MODEL), 1.0),
        'c': nrm((BATCH, D_MODEL), 1.0),
        'ctx': nrm((BATCH, CTX_LEN, D_MODEL), 1.0),
        'c_ctx': nrm((D_MODEL,), 1.0),
        'ada_w': lin((DEPTH, D_MODEL, 6 * D_MODEL), D_MODEL, 0.5),
        'ada_b': nrm((DEPTH, 6 * D_MODEL), 0.02),
        'norm_g': norm_gain((DEPTH, 4, D_MODEL)),
        'w_in_e': lin((n_even, D_MODEL, EVEN_IN), D_MODEL),
        'w_out_e': lin((n_even, EVEN_MIX, D_MODEL), EVEN_MIX),
        'na_rpb': nrm((n_even, NA_HEADS, 2 * NA_WIN_ROWS - 1, 2 * NA_WIN_COLS - 1), 0.2),
        'wg_sink': nrm((n_even, WG_HEADS), 0.5),
        'w_in_o': lin((n_odd, D_MODEL, ODD_IN), D_MODEL),
        'mla_q_norm': norm_gain((n_odd, MLA_Q_RANK)),
        'mla_kv_norm': norm_gain((n_odd, MLA_KV_RANK)),
        'mla_w_qb': lin((n_odd, MLA_Q_RANK, MLA_HEADS * (MLA_NOPE + MLA_ROPE)), MLA_Q_RANK),
        'mla_w_kvb': lin((n_odd, MLA_KV_RANK, MLA_HEADS * (MLA_NOPE + MLA_V)), MLA_KV_RANK),
        'w_out_o': lin((n_odd, ODD_MIX, D_MODEL), ODD_MIX),
        'w_router': lin((D_MODEL, N_EXPERTS), D_MODEL),
        'b_router': nrm((N_EXPERTS,), 0.01),
        'moe_w_gate': lin((DEPTH, N_EXPERTS, D_MODEL, D_EXPERT), D_MODEL),
        'moe_w_up': lin((DEPTH, N_EXPERTS, D_MODEL, D_EXPERT), D_MODEL),
        'moe_w_down': lin((DEPTH, N_EXPERTS, D_EXPERT, D_MODEL), D_EXPERT),
        'shared_w_gate': lin((DEPTH, D_MODEL, D_SHARED), D_MODEL),
        'shared_w_up': lin((DEPTH, D_MODEL, D_SHARED), D_MODEL),
        'shared_w_down': lin((DEPTH, D_SHARED, D_MODEL), D_SHARED),
    }


def reference(x, c, ctx, c_ctx, ada_w, ada_b, norm_g, w_in_e, w_out_e, na_rpb, wg_sink, w_in_o, mla_q_norm, mla_kv_norm, mla_w_qb, mla_w_kvb, w_out_o, w_router, b_router, moe_w_gate, moe_w_up, moe_w_down, shared_w_gate, shared_w_up, shared_w_down):
    n_tok = x.shape[1]
    cos_g, sin_g = axial_rope_tables(n_tok, HEAD_DIM)
    cos_m, sin_m = axial_rope_tables(n_tok, MLA_ROPE)
    s_c = jax.nn.silu(c)
    s_cc = jax.nn.silu(c_ctx)
    xc = ctx
    for layer in range(DEPTH):
        last = layer == DEPTH - 1
        mod = (s_c @ ada_w[layer] + ada_b[layer])[:, None, :]
        mod_c = s_cc @ ada_w[layer] + ada_b[layer]
        sh_a, sc_a, g_a, sh_f, sc_f, g_f = jnp.split(mod, 6, axis=-1)
        csh_a, csc_a, cg_a, csh_f, csc_f, cg_f = jnp.split(mod_c, 6, axis=-1)
        hx = rms_norm(x, norm_g[layer, 0]) * (1 + sc_a) + sh_a
        hc = rms_norm(xc, norm_g[layer, 0]) * (1 + csc_a) + csh_a
        i = layer // 2
        if layer % 2 == 0:
            yx, yc = even_mixer(hx, hc, w_in_e[i], w_out_e[i], na_rpb[i], wg_sink[i], cos_g, sin_g, not last)
        else:
            yx, yc = odd_mixer(hx, hc, w_in_o[i], mla_q_norm[i], mla_kv_norm[i], mla_w_qb[i], mla_w_kvb[i], w_out_o[i], cos_m, sin_m, not last)
        x = x + g_a * rms_norm(yx, norm_g[layer, 1])
        hx = rms_norm(x, norm_g[layer, 2]) * (1 + sc_f) + sh_f
        fx = moe_ffn(hx, w_router, b_router, moe_w_gate[layer], moe_w_up[layer], moe_w_down[layer], shared_w_gate[layer], shared_w_up[layer], shared_w_down[layer])
        x = x + g_f * rms_norm(fx, norm_g[layer, 3])
        if not last:
            xc = xc + cg_a * rms_norm(yc, norm_g[layer, 1])
            hc = rms_norm(xc, norm_g[layer, 2]) * (1 + csc_f) + csh_f
            fc = moe_ffn(hc, w_router, b_router, moe_w_gate[layer], moe_w_up[layer], moe_w_down[layer], shared_w_gate[layer], shared_w_up[layer], shared_w_down[layer])
            xc = xc + cg_f * rms_norm(fc, norm_g[layer, 3])
    return x
```

```python
import functools

import jax
import jax.numpy as jnp
from jax import lax
from jax.experimental import pallas as pl
from jax.experimental.pallas import tpu as pltpu

F32 = jnp.float32
BF16 = jnp.bfloat16
I32 = jnp.int32

D = 2048
GRID_W = 64
HD = 128
EPS = 1e-6
NEG = -1e30
ROPE_THETA = 10000.0
LOG2E = 1.4426950408889634

NA_HEADS = 8
NA_WIN_ROWS = 8
NA_WIN_COLS = 16
NA_QROWS = 4
NA_KROWS = NA_QROWS + NA_WIN_ROWS
WG_HEADS = 8
WG_KV = 2
WG_GRP = WG_HEADS // WG_KV
WG_WINDOW = 128
WG_TQ = 256
WG_SPAN = WG_TQ + 2 * WG_WINDOW
EVEN_IN = (3 * NA_HEADS + WG_HEADS + 2 * WG_KV) * HD
MLA_HEADS = 16
MLA_RANK = 512
MLA_NOPE = 128
MLA_ROPE = 64
MLA_V = 128
MLA_QK = 256
MLA_TQ = 512
MLA_TK = 512
N_EXPERTS = 16
N_GROUPS = 4
GROUP_SZ = N_EXPERTS // N_GROUPS
D_EXPERT = 512
D_SHARED = 1024

TM = 256
TMG = 256
VMEM_LIMIT = 56 * 1024 * 1024


def _params(sem, vmem=VMEM_LIMIT):
    return pltpu.CompilerParams(dimension_semantics=sem, vmem_limit_bytes=vmem)


def _dot(a, b):
    return jnp.dot(a, b, preferred_element_type=F32)


def _dot_nt(a, b):
    return lax.dot_general(a, b, (((1,), (1,)), ((), ())), preferred_element_type=F32)


def _resident(shape):
    nd = len(shape)
    return pl.BlockSpec(shape, lambda *_: (0,) * nd, pipeline_mode=pl.Buffered(1))


def _silu(a):
    return a * jax.nn.sigmoid(a)


def _rms(x):
    return x * lax.rsqrt(jnp.mean(x * x, axis=-1, keepdims=True) + EPS)


def _ada_kernel(c_ref, w_ref, b_ref, o_ref):
    s = _silu(c_ref[...]).astype(BF16)
    o_ref[...] = _dot(s, w_ref[...].astype(BF16)) + b_ref[...]


def _ada_mods(cc, ada_w, ada_b):
    n_layers, _, n_out = ada_w.shape
    tn = 1024
    out = pl.pallas_call(
        _ada_kernel,
        grid=(n_layers, n_out // tn),
        in_specs=[
            pl.BlockSpec((8, D), lambda l, j: (0, 0)),
            pl.BlockSpec((None, D, tn), lambda l, j: (l, 0, j)),
            pl.BlockSpec((None, 1, tn), lambda l, j: (l, 0, j)),
        ],
        out_specs=pl.BlockSpec((None, 8, tn), lambda l, j: (l, 0, j)),
        out_shape=jax.ShapeDtypeStruct((n_layers, 8, n_out), F32),
        compiler_params=_params(("parallel", "parallel")),
    )(cc, ada_w, ada_b.reshape(n_layers, 1, n_out))
    return out.reshape(n_layers, 8, 6, 1, D)


def _mod_spec(which, n_lat_tiles, ctx_row):
    return pl.BlockSpec(
        (None, None, 1, D),
        lambda b, t: (jnp.where(t >= n_lat_tiles, ctx_row, b), which, 0, 0),
    )


def _norm_mod(x, g, sc, sh):
    return (_rms(x) * g) * (1.0 + sc) + sh


def _norm_first_kernel(x_ref, c_ref, g_ref, sc_ref, sh_ref, xs_ref, h_ref, *, n_lat):
    t = pl.program_id(1)

    def emit(src):
        x = src[...]
        xs_ref[...] = x
        h_ref[...] = _norm_mod(x, g_ref[...], sc_ref[...], sh_ref[...]).astype(BF16)

    @pl.when(t < n_lat)
    def _():
        emit(x_ref)

    @pl.when(t == n_lat)
    def _():
        emit(c_ref)


def _norm_first(x, ctx, gain, mods, n_batch):
    b_sz, s_len, _ = x.shape
    c_len = ctx.shape[1]
    n_lat = s_len // TM
    t_len = s_len + c_len
    row = lambda b, t: (b, t, 0)
    return pl.pallas_call(
        functools.partial(_norm_first_kernel, n_lat=n_lat),
        grid=(b_sz, n_lat + 1),
        in_specs=[
            pl.BlockSpec((None, TM, D), lambda b, t: (b, jnp.minimum(t, n_lat - 1), 0)),
            pl.BlockSpec((None, TM, D), lambda b, t: (b, 0, 0)),
            pl.BlockSpec((1, D), lambda b, t: (0, 0)),
            _mod_spec(1, n_lat, n_batch),
            _mod_spec(0, n_lat, n_batch),
        ],
        out_specs=[pl.BlockSpec((None, TM, D), row), pl.BlockSpec((None, TM, D), row)],
        out_shape=[
            jax.ShapeDtypeStruct((b_sz, t_len, D), F32),
            jax.ShapeDtypeStruct((b_sz, t_len, D), BF16),
        ],
        compiler_params=_params(("parallel", "arbitrary")),
    )(x, ctx, gain.reshape(1, D), mods, mods)


def _norm_kernel(x_ref, g_ref, sc_ref, sh_ref, *o_refs):
    h = _norm_mod(x_ref[...], g_ref[...], sc_ref[...], sh_ref[...])
    for o_ref in o_refs:
        o_ref[...] = h.astype(o_ref.dtype)


def _norm(xs, gain, mods, sc_idx, sh_idx, n_lat, n_batch, out_dtypes):
    b_sz, t_len, _ = xs.shape
    row = lambda b, t: (b, t, 0)
    return pl.pallas_call(
        _norm_kernel,
        grid=(b_sz, t_len // TM),
        in_specs=[
            pl.BlockSpec((None, TM, D), row),
            pl.BlockSpec((1, D), lambda b, t: (0, 0)),
            _mod_spec(sc_idx, n_lat, n_batch),
            _mod_spec(sh_idx, n_lat, n_batch),
        ],
        out_specs=[pl.BlockSpec((None, TM, D), row) for _ in out_dtypes],
        out_shape=[jax.ShapeDtypeStruct((b_sz, t_len, D), dt) for dt in out_dtypes],
        compiler_params=_params(("parallel", "parallel")),
    )(xs, gain.reshape(1, D), mods, mods)


def _rope_tables(s_len, c_len, rot_dim):
    quarter = rot_dim // 4
    t = jnp.arange(s_len, dtype=I32)
    row = (t // GRID_W).astype(F32)
    col = (t % GRID_W).astype(F32)
    inv_freq = ROPE_THETA ** (-jnp.arange(quarter, dtype=F32) / quarter)
    ang_r = row[:, None] * inv_freq[None, :]
    ang_c = col[:, None] * inv_freq[None, :]
    ang = jnp.concatenate([ang_r, ang_r, ang_c, ang_c], axis=-1)
    cos, sin = jnp.cos(ang), jnp.sin(ang)
    even = ((jnp.arange(rot_dim) // quarter) % 2 == 0)[None, :]
    sa = jnp.where(even, -sin, 0.0)
    sb = jnp.where(even, 0.0, sin)
    pad = ((0, c_len), (0, HD - rot_dim))
    return (
        jnp.pad(cos, pad, constant_values=1.0),
        jnp.pad(sa, pad),
        jnp.pad(sb, pad),
    )


def _rope(x, cos, sa, sb, quarter):
    return x * cos + pltpu.roll(x, HD - quarter, 1) * sa + pltpu.roll(x, quarter, 1) * sb


EVEN_TN = 1536
EVEN_ROPE_BLOCK = 2
EVEN_ROPE_CHUNKS = WG_HEADS + WG_KV


def _inproj_even_kernel(h_ref, w_ref, cos_ref, sa_ref, sb_ref, o_ref):
    j = pl.program_id(0)
    acc = _dot(h_ref[...], w_ref[...])

    @pl.when(j != EVEN_ROPE_BLOCK)
    def _():
        o_ref[...] = acc.astype(BF16)

    @pl.when(j == EVEN_ROPE_BLOCK)
    def _():
        cos, sa, sb = cos_ref[...], sa_ref[...], sb_ref[...]
        for ch in range(EVEN_ROPE_CHUNKS):
            sl = slice(ch * HD, (ch + 1) * HD)
            o_ref[:, sl] = _rope(acc[:, sl], cos, sa, sb, HD // 4).astype(BF16)
        rest = slice(EVEN_ROPE_CHUNKS * HD, EVEN_TN)
        o_ref[:, rest] = acc[:, rest].astype(BF16)


def _inproj_even(h, w, tables):
    b_sz, t_len, _ = h.shape
    n_t = t_len // TM
    rows = b_sz * t_len
    tab = pl.BlockSpec((TM, HD), lambda j, m: (m % n_t, 0))
    out = pl.pallas_call(
        _inproj_even_kernel,
        grid=(EVEN_IN // EVEN_TN, rows // TM),
        in_specs=[
            pl.BlockSpec((TM, D), lambda j, m: (m, 0)),
            pl.BlockSpec((D, EVEN_TN), lambda j, m: (0, j)),
            tab, tab, tab,
        ],
        out_specs=pl.BlockSpec((TM, EVEN_TN), lambda j, m: (m, j)),
        out_shape=jax.ShapeDtypeStruct((rows, EVEN_IN), BF16),
        compiler_params=_params(("parallel", "parallel")),
    )(h.reshape(rows, D), w, *tables)
    return out.reshape(b_sz, t_len, EVEN_IN)


def _na_bias_tables(rpb, rows):
    n_tiles = rows // NA_QROWS
    tabs = []
    for i in (0, 1, n_tiles - 1):
        r0 = min(max(NA_QROWS * i - NA_WIN_ROWS // 2, 0), rows - NA_KROWS)
        qr = NA_QROWS * i + jnp.arange(NA_QROWS)
        kr = r0 + jnp.arange(NA_KROWS)
        qc = jnp.arange(GRID_W)
        kc = jnp.arange(GRID_W)
        r_start = jnp.clip(qr - NA_WIN_ROWS // 2, 0, rows - NA_WIN_ROWS)
        c_start = jnp.clip(qc - NA_WIN_COLS // 2, 0, GRID_W - NA_WIN_COLS)
        row_ok = (kr[None, :] >= r_start[:, None]) & (kr[None, :] < r_start[:, None] + NA_WIN_ROWS)
        col_ok = (kc[None, :] >= c_start[:, None]) & (kc[None, :] < c_start[:, None] + NA_WIN_COLS)
        dr = jnp.clip(kr[None, :] - qr[:, None] + NA_WIN_ROWS - 1, 0, 2 * NA_WIN_ROWS - 2)
        dc = jnp.clip(kc[None, :] - qc[:, None] + NA_WIN_COLS - 1, 0, 2 * NA_WIN_COLS - 2)
        bias = rpb.astype(F32)[:, dr[:, None, :, None], dc[None, :, None, :]]
        ok = row_ok[:, None, :, None] & col_ok[None, :, None, :]
        tabs.append(jnp.where(ok[None], bias, NEG).reshape(
            rpb.shape[0], NA_QROWS * GRID_W, NA_KROWS * GRID_W))
    return jnp.stack(tabs, axis=1)


def _softmax_pv(parts, extra_logit=None):
    m = parts[0][0].max(axis=-1, keepdims=True)
    for s, _ in parts[1:]:
        m = jnp.maximum(m, s.max(axis=-1, keepdims=True))
    if extra_logit is not None:
        m = jnp.maximum(m, extra_logit)
    l = None
    o = None
    for s, v in parts:
        p = jnp.exp(s - m)
        ls = p.sum(axis=-1, keepdims=True)
        os_ = _dot(p.astype(BF16), v)
        l = ls if l is None else l + ls
        o = os_ if o is None else o + os_
    if extra_logit is not None:
        l = l + jnp.exp(extra_logit - m)
    return o / l


def _na_kernel(q_ref, k_ref, v_ref, bias_ref, o_ref, *, s_len, c_len):
    rows = s_len // GRID_W
    tq = NA_QROWS * GRID_W
    tk = NA_KROWS * GRID_W
    n_q = s_len // tq
    scale = HD ** -0.5
    kc = k_ref[s_len:s_len + c_len, :]
    vc = v_ref[s_len:s_len + c_len, :]

    def body(i, carry):
        qs = pl.multiple_of(i * tq, tq)
        r0 = jnp.clip(NA_QROWS * i - NA_WIN_ROWS // 2, 0, rows - NA_KROWS)
        ks = pl.multiple_of(r0 * GRID_W, tq)
        typ = jnp.where(i == 0, 0, jnp.where(i == n_q - 1, 2, 1))
        q = q_ref[pl.ds(qs, tq), :]
        s_loc = _dot_nt(q, k_ref[pl.ds(ks, tk), :]) * scale + bias_ref[typ]
        s_ctx = _dot_nt(q, kc) * scale
        o = _softmax_pv([(s_loc, v_ref[pl.ds(ks, tk), :]), (s_ctx, vc)])
        o_ref[pl.ds(qs, tq), :] = o.astype(BF16)
        return carry

    lax.fori_loop(0, n_q, body, 0)
    s_cc = _dot_nt(q_ref[s_len:s_len + c_len, :], kc) * scale
    o_ref[s_len:s_len + c_len, :] = _softmax_pv([(s_cc, vc)]).astype(BF16)


def _na_attention(proj, bias, s_len, c_len):
    b_sz, t_len, _ = proj.shape
    col = lambda off: pl.BlockSpec((None, t_len, HD), lambda b, h: (b, 0, off + h))
    return pl.pallas_call(
        functools.partial(_na_kernel, s_len=s_len, c_len=c_len),
        grid=(b_sz, NA_HEADS),
        in_specs=[
            col(0), col(NA_HEADS), col(2 * NA_HEADS),
            pl.BlockSpec((None,) + bias.shape[1:], lambda b, h: (h, 0, 0, 0)),
        ],
        out_specs=pl.BlockSpec((None, t_len, HD), lambda b, h: (b, 0, h)),
        out_shape=jax.ShapeDtypeStruct((b_sz, t_len, NA_HEADS * HD), BF16),
        compiler_params=_params(("parallel", "parallel")),
    )(proj, proj, proj, bias)


def _wg_mask_tables(s_len):
    n_q = s_len // WG_TQ
    tabs = []
    for i in (0, 1, n_q - 1):
        k0 = min(max(WG_TQ * i - WG_WINDOW, 0), s_len - WG_SPAN)
        rel = (k0 + jnp.arange(WG_SPAN))[None, :] - (WG_TQ * i + jnp.arange(WG_TQ))[:, None]
        tabs.append(jnp.where(jnp.abs(rel) <= WG_WINDOW, 0.0, NEG).astype(F32))
    return jnp.stack(tabs)


def _wg_kernel(sink_ref, q_ref, k_ref, v_ref, mask_ref, o_ref, *, s_len, c_len):
    kvh = pl.program_id(1)
    n_q = s_len // WG_TQ
    scale = HD ** -0.5
    kc = k_ref[s_len:s_len + c_len, :]
    vc = v_ref[s_len:s_len + c_len, :]

    def body(i, carry):
        qs = pl.multiple_of(i * WG_TQ, WG_TQ)
        ks = pl.multiple_of(jnp.clip(WG_TQ * i - WG_WINDOW, 0, s_len - WG_SPAN), WG_WINDOW)
        typ = jnp.where(i == 0, 0, jnp.where(i == n_q - 1, 2, 1))
        kl = k_ref[pl.ds(ks, WG_SPAN), :]
        vl = v_ref[pl.ds(ks, WG_SPAN), :]
        mask = mask_ref[typ]
        for g in range(WG_GRP):
            sl = slice(g * HD, (g + 1) * HD)
            q = q_ref[pl.ds(qs, WG_TQ), sl]
            s_loc = _dot_nt(q, kl) * scale + mask
            s_ctx = _dot_nt(q, kc) * scale
            o = _softmax_pv([(s_loc, vl), (s_ctx, vc)], sink_ref[kvh * WG_GRP + g])
            o_ref[pl.ds(qs, WG_TQ), sl] = o.astype(BF16)
        return carry

    lax.fori_loop(0, n_q, body, 0)
    for g in range(WG_GRP):
        sl = slice(g * HD, (g + 1) * HD)
        s_cc = _dot_nt(q_ref[s_len:s_len + c_len, sl], kc) * scale
        o = _softmax_pv([(s_cc, vc)], sink_ref[kvh * WG_GRP + g])
        o_ref[s_len:s_len + c_len, sl] = o.astype(BF16)


def _wg_attention(proj, sink, mask, s_len, c_len):
    b_sz, t_len, _ = proj.shape
    q_off = 3 * NA_HEADS * HD // (WG_GRP * HD)
    k_off = 3 * NA_HEADS + WG_HEADS
    v_off = k_off + WG_KV
    return pl.pallas_call(
        functools.partial(_wg_kernel, s_len=s_len, c_len=c_len),
        grid=(b_sz, WG_KV),
        in_specs=[
            pl.BlockSpec(memory_space=pltpu.SMEM),
            pl.BlockSpec((None, t_len, WG_GRP * HD), lambda b, h: (b, 0, q_off + h)),
            pl.BlockSpec((None, t_len, HD), lambda b, h: (b, 0, k_off + h)),
            pl.BlockSpec((None, t_len, HD), lambda b, h: (b, 0, v_off + h)),
            pl.BlockSpec(mask.shape, lambda b, h: (0, 0, 0)),
        ],
        out_specs=pl.BlockSpec((None, t_len, WG_GRP * HD), lambda b, h: (b, 0, h)),
        out_shape=jax.ShapeDtypeStruct((b_sz, t_len, WG_HEADS * HD), BF16),
        compiler_params=_params(("parallel", "parallel")),
    )(sink.astype(F32), proj, proj, proj, mask)


def _outproj_kernel(*refs, n_parts):
    y_refs = refs[:n_parts]
    w_ref, x_ref, gate_ref, gain_ref, o_ref = refs[n_parts:]
    y = None
    off = 0
    for y_ref in y_refs:
        k = y_ref.shape[-1]
        part = _dot(y_ref[...], w_ref[off:off + k, :])
        y = part if y is None else y + part
        off += k
    o_ref[...] = x_ref[...] + gate_ref[...] * (_rms(y) * gain_ref[...])


def _outproj(ys, w, xs, gain, mods, gate_idx, n_rows, n_lat, n_batch):
    b_sz = xs.shape[0]
    row = lambda b, t: (b, t, 0)
    return pl.pallas_call(
        functools.partial(_outproj_kernel, n_parts=len(ys)),
        grid=(b_sz, n_rows // TM),
        in_specs=[pl.BlockSpec((None, TM, y.shape[-1]), row) for y in ys] + [
            _resident(w.shape),
            pl.BlockSpec((None, TM, D), row),
            _mod_spec(gate_idx, n_lat, n_batch),
            pl.BlockSpec((1, D), lambda b, t: (0, 0)),
        ],
        out_specs=pl.BlockSpec((None, TM, D), row),
        out_shape=jax.ShapeDtypeStruct((b_sz, n_rows, D), F32),
        compiler_params=_params(("parallel", "parallel")),
    )(*ys, w, xs, mods, gain.reshape(1, D))


def _mla_in_kernel(h_ref, win_ref, wqb_ref, wkn_ref, wv_ref, qg_ref, kvg_ref,
                   cos_ref, sa_ref, sb_ref, q_ref, k_ref, v_ref):
    cos, sa, sb = cos_ref[...], sa_ref[...], sb_ref[...]
    rope = lambda x: _rope(x, cos, sa, sb, MLA_ROPE // 4)
    c = _dot(h_ref[...], win_ref[...])
    cq = c[:, :MLA_RANK]
    ckv = c[:, MLA_RANK:2 * MLA_RANK]
    k_rope = rope(c[:, 2 * MLA_RANK:]).astype(BF16)
    q_scale = (MLA_NOPE + MLA_ROPE) ** -0.5 * LOG2E
    q = _dot((_rms(cq) * qg_ref[...]).astype(BF16), wqb_ref[...])
    kvn = (_rms(ckv) * kvg_ref[...]).astype(BF16)
    k_nope = _dot(kvn, wkn_ref[...])
    v_ref[...] = _dot(kvn, wv_ref[...]).astype(BF16)
    for h in range(MLA_HEADS):
        lo = slice(h * MLA_QK, h * MLA_QK + HD)
        hi = slice(h * MLA_QK + HD, (h + 1) * MLA_QK)
        q_ref[:, lo] = (q[:, lo] * q_scale).astype(BF16)
        q_ref[:, hi] = (rope(q[:, hi]) * q_scale).astype(BF16)
        k_ref[:, lo] = k_nope[:, h * HD:(h + 1) * HD].astype(BF16)
        k_ref[:, hi] = k_rope


def _mla_in(h, win, wqb, wkn, wv, q_gain, kv_gain, tables):
    b_sz, t_len, _ = h.shape
    n_t = t_len // TM
    rows = b_sz * t_len
    tab = pl.BlockSpec((TM, HD), lambda m: (m % n_t, 0))
    tile = lambda n: pl.BlockSpec((TM, n), lambda m: (m, 0))
    q, k, v = pl.pallas_call(
        _mla_in_kernel,
        grid=(rows // TM,),
        in_specs=[
            tile(D), _resident(win.shape), _resident(wqb.shape), _resident(wkn.shape),
            _resident(wv.shape), _resident((1, MLA_RANK)), _resident((1, MLA_RANK)),
            tab, tab, tab,
        ],
        out_specs=[tile(MLA_HEADS * MLA_QK), tile(MLA_HEADS * MLA_QK), tile(MLA_HEADS * MLA_V)],
        out_shape=[
            jax.ShapeDtypeStruct((rows, MLA_HEADS * MLA_QK), BF16),
            jax.ShapeDtypeStruct((rows, MLA_HEADS * MLA_QK), BF16),
            jax.ShapeDtypeStruct((rows, MLA_HEADS * MLA_V), BF16),
        ],
        compiler_params=_params(("parallel",)),
    )(h.reshape(rows, D), win, wqb, wkn, wv, q_gain.reshape(1, MLA_RANK),
      kv_gain.reshape(1, MLA_RANK), *tables)
    return (q.reshape(b_sz, t_len, -1), k.reshape(b_sz, t_len, -1), v.reshape(b_sz, t_len, -1))


def _mla_attn_kernel(q_ref, k_ref, v_ref, o_ref, *, s_len, c_len):
    q = q_ref[...]

    def step(kt, vt, m, l, acc):
        s = _dot_nt(q, kt)
        m_new = jnp.maximum(m, s.max(axis=-1, keepdims=True))
        alpha = jnp.exp2(m - m_new)
        p = jnp.exp2(s - m_new)
        l = alpha * l + p.sum(axis=-1, keepdims=True)
        acc = alpha * acc + _dot(p.astype(BF16), vt)
        return m_new, l, acc

    def body(j, carry):
        ks = pl.multiple_of(j * MLA_TK, MLA_TK)
        return step(k_ref[pl.ds(ks, MLA_TK), :], v_ref[pl.ds(ks, MLA_TK), :], *carry)

    init = (
        jnp.full((MLA_TQ, 1), -jnp.inf, F32),
        jnp.zeros((MLA_TQ, 1), F32),
        jnp.zeros((MLA_TQ, MLA_V), F32),
    )
    carry = lax.fori_loop(0, s_len // MLA_TK, body, init)
    _, l, acc = step(k_ref[s_len:s_len + c_len, :], v_ref[s_len:s_len + c_len, :], *carry)
    o_ref[...] = (acc / l).astype(BF16)


def _mla_attention(q, k, v, s_len, c_len):
    b_sz, t_len, _ = q.shape
    return pl.pallas_call(
        functools.partial(_mla_attn_kernel, s_len=s_len, c_len=c_len),
        grid=(b_sz, MLA_HEADS, s_len // MLA_TQ),
        in_specs=[
            pl.BlockSpec((None, MLA_TQ, MLA_QK), lambda b, h, i: (b, i, h)),
            pl.BlockSpec((None, t_len, MLA_QK), lambda b, h, i: (b, 0, h)),
            pl.BlockSpec((None, t_len, MLA_V), lambda b, h, i: (b, 0, h)),
        ],
        out_specs=pl.BlockSpec((None, MLA_TQ, MLA_V), lambda b, h, i: (b, i, h)),
        out_shape=jax.ShapeDtypeStruct((b_sz, s_len, MLA_HEADS * MLA_V), BF16),
        compiler_params=_params(("parallel", "parallel", "parallel")),
    )(q, k, v)


def _router_kernel(h_ref, wr_ref, br_ref, idx_ref, wt_ref):
    scores = jax.nn.sigmoid(_dot_nt(wr_ref[...], h_ref[...]))
    sel = scores + br_ref[...]
    sel_rows = [sel[e:e + 1, :] for e in range(N_EXPERTS)]
    score_rows = [scores[e:e + 1, :] for e in range(N_EXPERTS)]

    group_scores = []
    for g in range(N_GROUPS):
        a, b, c, d = sel_rows[g * GROUP_SZ:(g + 1) * GROUP_SZ]
        hi1, lo1 = jnp.maximum(a, b), jnp.minimum(a, b)
        hi2, lo2 = jnp.maximum(c, d), jnp.minimum(c, d)
        top1 = jnp.maximum(hi1, hi2)
        top2 = jnp.maximum(jnp.minimum(hi1, hi2), jnp.maximum(lo1, lo2))
        group_scores.append(top1 + top2)
    best = group_scores[0]
    gi = jnp.zeros_like(best, dtype=I32)
    for g in range(1, N_GROUPS):
        upd = group_scores[g] > best
        gi = jnp.where(upd, g, gi)
        best = jnp.where(upd, group_scores[g], best)

    def pick_group(rows_, j):
        out = rows_[j]
        for g in range(1, N_GROUPS):
            out = jnp.where(gi == g, rows_[g * GROUP_SZ + j], out)
        return out

    gsel = [pick_group(sel_rows, j) for j in range(GROUP_SZ)]
    gscore = [pick_group(score_rows, j) for j in range(GROUP_SZ)]

    def argmax_excluding(skip):
        val = jnp.full_like(best, -jnp.inf)
        idx = jnp.zeros_like(gi)
        wt = jnp.zeros_like(best)
        for j in range(GROUP_SZ):
            cand = gsel[j] if skip is None else jnp.where(skip == j, -jnp.inf, gsel[j])
            upd = cand > val
            idx = jnp.where(upd, j, idx)
            wt = jnp.where(upd, gscore[j], wt)
            val = jnp.where(upd, cand, val)
        return idx, wt

    i1, w1 = argmax_excluding(None)
    i2, w2 = argmax_excluding(i1)
    tot = w1 + w2
    idx_ref[0:1, :] = gi * GROUP_SZ + i1
    idx_ref[1:2, :] = gi * GROUP_SZ + i2
    wt_ref[0:1, :] = w1 / tot
    wt_ref[1:2, :] = w2 / tot


def _router(h16, wr_t, br):
    rows = h16.shape[0]
    return pl.pallas_call(
        _router_kernel,
        grid=(rows // TM,),
        in_specs=[
            pl.BlockSpec((TM, D), lambda m: (m, 0)),
            pl.BlockSpec((N_EXPERTS, D), lambda m: (0, 0)),
            pl.BlockSpec((N_EXPERTS, 1), lambda m: (0, 0)),
        ],
        out_specs=[pl.BlockSpec((2, TM), lambda m: (0, m)), pl.BlockSpec((2, TM), lambda m: (0, m))],
        out_shape=[jax.ShapeDtypeStruct((2, rows), I32), jax.ShapeDtypeStruct((2, rows), F32)],
        compiler_params=_params(("parallel",)),
    )(h16, wr_t, br.reshape(N_EXPERTS, 1).astype(F32))


def _dispatch_plan(idx, rows):
    n_tiles = 2 * rows // TMG + N_EXPERTS
    e = idx.reshape(-1)
    onehot = (e[:, None] == jnp.arange(N_EXPERTS, dtype=I32)[None, :]).astype(I32)
    csum = jnp.cumsum(onehot, axis=0)
    rank = jnp.take_along_axis(csum, e[:, None], axis=1)[:, 0] - 1
    counts = csum[-1]
    padded = (counts + TMG - 1) // TMG * TMG
    ends = jnp.cumsum(padded)
    dest = (ends - padded)[e] + rank
    tok = jnp.tile(jnp.arange(rows, dtype=I32), 2)
    src = jnp.zeros((n_tiles * TMG,), I32).at[dest].set(tok)
    tile_start = jnp.arange(n_tiles, dtype=I32) * TMG
    tile_expert = jnp.minimum(jnp.searchsorted(ends, tile_start, side="right"), N_EXPERTS - 1).astype(I32)
    tile_valid = (tile_start < ends[-1]).astype(I32)
    return src, dest.reshape(2, rows).astype(I32), tile_expert, tile_valid, n_tiles


def _gmm_kernel(te_ref, tv_ref, src_ref, h_hbm, wg_ref, wu_ref, wd_ref, y_ref, buf, sem):
    del te_ref
    i = pl.program_id(0)
    n = pl.num_programs(0)

    def gather(tile, slot):
        base = tile * TMG

        def body(r, c):
            pltpu.make_async_copy(
                h_hbm.at[pl.ds(src_ref[base + r], 1)], buf.at[slot, pl.ds(r, 1)], sem.at[slot]
            ).start()
            return c

        lax.fori_loop(0, TMG, body, 0, unroll=8)

    @pl.when((i == 0) & (tv_ref[0] > 0))
    def _():
        gather(0, 0)

    nxt = jnp.minimum(i + 1, n - 1)

    @pl.when((i + 1 < n) & (tv_ref[nxt] > 0))
    def _():
        gather(i + 1, (i + 1) % 2)

    slot = i % 2

    @pl.when(tv_ref[i] > 0)
    def _():
        pltpu.make_async_copy(h_hbm.at[pl.ds(0, TMG)], buf.at[slot], sem.at[slot]).wait()
        x = buf[slot].astype(BF16)
        z = _silu(_dot(x, wg_ref[...])) * _dot(x, wu_ref[...])
        y_ref[...] = _dot(z.astype(BF16), wd_ref[...])

    @pl.when(tv_ref[i] == 0)
    def _():
        y_ref[...] = jnp.zeros_like(y_ref)


def _gmm(h32, src, tile_expert, tile_valid, n_tiles, wg, wu, wd):
    expert = lambda i, te, tv, src_: (te[i], 0, 0)
    grid_spec = pltpu.PrefetchScalarGridSpec(
        num_scalar_prefetch=3,
        grid=(n_tiles,),
        in_specs=[
            pl.BlockSpec(memory_space=pl.ANY),
            pl.BlockSpec((None, D, D_EXPERT), expert),
            pl.BlockSpec((None, D, D_EXPERT), expert),
            pl.BlockSpec((None, D_EXPERT, D), expert),
        ],
        out_specs=pl.BlockSpec((TMG, D), lambda i, te, tv, src_: (i, 0)),
        scratch_shapes=[pltpu.VMEM((2, TMG, D), F32), pltpu.SemaphoreType.DMA((2,))],
    )
    return pl.pallas_call(
        _gmm_kernel,
        grid_spec=grid_spec,
        out_shape=jax.ShapeDtypeStruct((n_tiles * TMG, D), F32),
        compiler_params=_params(("arbitrary",)),
    )(tile_expert, tile_valid, src, h32, wg, wu, wd)


def _ffn_kernel(d0_ref, d1_ref, h_ref, wsg_ref, wsu_ref, wsd_ref, y_hbm, wt_ref, x_ref,
                gate_ref, gain_ref, o_ref, buf, sem, *, n_t):
    base = (pl.program_id(0) * n_t + pl.program_id(1)) * TM

    def body(r, c):
        pltpu.make_async_copy(y_hbm.at[pl.ds(d0_ref[base + r], 1)], buf.at[0, pl.ds(r, 1)], sem.at[0]).start()
        pltpu.make_async_copy(y_hbm.at[pl.ds(d1_ref[base + r], 1)], buf.at[1, pl.ds(r, 1)], sem.at[1]).start()
        return c

    lax.fori_loop(0, TM, body, 0, unroll=8)
    h = h_ref[...]
    z = _silu(_dot(h, wsg_ref[...])) * _dot(h, wsu_ref[...])
    shared = _dot(z.astype(BF16), wsd_ref[...])
    for s in range(2):
        pltpu.make_async_copy(y_hbm.at[pl.ds(0, TM)], buf.at[s], sem.at[s]).wait()
    wt = wt_ref[...]
    fx = wt[:, 0:1] * buf[0] + wt[:, 1:2] * buf[1] + shared
    o_ref[...] = x_ref[...] + gate_ref[...] * (_rms(fx) * gain_ref[...])


def _ffn_combine(h16, y, dest, wts, xs, wsg, wsu, wsd, gain, mods, n_lat, n_batch):
    b_sz, t_len, _ = xs.shape
    n_t = t_len // TM
    row = lambda b, t, d0, d1: (b, t, 0)
    const2 = lambda b, t, d0, d1: (0, 0)
    mod = pl.BlockSpec(
        (None, None, 1, D),
        lambda b, t, d0, d1: (jnp.where(t >= n_lat, n_batch, b), 5, 0, 0),
    )
    grid_spec = pltpu.PrefetchScalarGridSpec(
        num_scalar_prefetch=2,
        grid=(b_sz, n_t),
        in_specs=[
            pl.BlockSpec((None, TM, D), row),
            pl.BlockSpec(wsg.shape, const2, pipeline_mode=pl.Buffered(1)),
            pl.BlockSpec(wsu.shape, const2, pipeline_mode=pl.Buffered(1)),
            pl.BlockSpec(wsd.shape, const2, pipeline_mode=pl.Buffered(1)),
            pl.BlockSpec(memory_space=pl.ANY),
            pl.BlockSpec((None, TM, 2), row),
            pl.BlockSpec((None, TM, D), row),
            mod,
            pl.BlockSpec((1, D), const2),
        ],
        out_specs=pl.BlockSpec((None, TM, D), row),
        scratch_shapes=[pltpu.VMEM((2, TM, D), F32), pltpu.SemaphoreType.DMA((2,))],
    )
    return pl.pallas_call(
        functools.partial(_ffn_kernel, n_t=n_t),
        grid_spec=grid_spec,
        out_shape=jax.ShapeDtypeStruct((b_sz, t_len, D), F32),
        compiler_params=_params(("arbitrary", "arbitrary")),
    )(dest[0], dest[1], h16, wsg, wsu, wsd, y, wts.T.reshape(b_sz, t_len, 2), xs, mods,
      gain.reshape(1, D))


def _moe_block(xs, gains, mods, n_lat, n_batch, wr_t, br, wg, wu, wd, wsg, wsu, wsd):
    b_sz, t_len, _ = xs.shape
    rows = b_sz * t_len
    h32, h16 = _norm(xs, gains[2], mods, 4, 3, n_lat, n_batch, (F32, BF16))
    idx, wts = _router(h16.reshape(rows, D), wr_t, br)
    src, dest, tile_expert, tile_valid, n_tiles = _dispatch_plan(idx, rows)
    y = _gmm(h32.reshape(rows, D), src, tile_expert, tile_valid, n_tiles, wg, wu, wd)
    return _ffn_combine(h16, y, dest, wts, xs, wsg, wsu, wsd, gains[3], mods, n_lat, n_batch)


def kernel(x, c, ctx, c_ctx, ada_w, ada_b, norm_g, w_in_e, w_out_e, na_rpb, wg_sink, w_in_o, mla_q_norm, mla_kv_norm, mla_w_qb, mla_w_kvb, w_out_o, w_router, b_router, moe_w_gate, moe_w_up, moe_w_down, shared_w_gate, shared_w_up, shared_w_down):
    b_sz, s_len, _ = x.shape
    c_len = ctx.shape[1]
    assert c_len == TM and s_len % MLA_TQ == 0 and b_sz < 8
    n_lat = s_len // TM
    bf = lambda a: a.astype(BF16)

    cc = jnp.zeros((8, D), F32).at[:b_sz].set(c).at[b_sz].set(c_ctx)
    mods = _ada_mods(cc, ada_w, ada_b)
    wr_t = bf(w_router.T)
    moe = lambda layer: (bf(moe_w_gate[layer]), bf(moe_w_up[layer]), bf(moe_w_down[layer]),
                         bf(shared_w_gate[layer]), bf(shared_w_up[layer]), bf(shared_w_down[layer]))

    xs, h = _norm_first(x, ctx, norm_g[0, 0], mods[0], b_sz)
    proj = _inproj_even(h, bf(w_in_e[0]), _rope_tables(s_len, c_len, HD))
    ya = _na_attention(proj, _na_bias_tables(na_rpb[0], s_len // GRID_W), s_len, c_len)
    yw = _wg_attention(proj, wg_sink[0], _wg_mask_tables(s_len), s_len, c_len)
    xs = _outproj([ya, yw], bf(w_out_e[0]), xs, norm_g[0, 1], mods[0], 2, s_len + c_len, n_lat, b_sz)
    xs = _moe_block(xs, norm_g[0], mods[0], n_lat, b_sz, wr_t, b_router, *moe(0))

    (h,) = _norm(xs, norm_g[1, 0], mods[1], 1, 0, n_lat, b_sz, (BF16,))
    win = bf(jnp.pad(w_in_o[0], ((0, 0), (0, HD - MLA_ROPE))))
    wqb = bf(jnp.pad(mla_w_qb[0].reshape(MLA_RANK, MLA_HEADS, MLA_NOPE + MLA_ROPE),
                     ((0, 0), (0, 0), (0, MLA_QK - MLA_NOPE - MLA_ROPE))).reshape(MLA_RANK, -1))
    wkvb = mla_w_kvb[0].reshape(MLA_RANK, MLA_HEADS, MLA_NOPE + MLA_V)
    wkn = bf(wkvb[:, :, :MLA_NOPE].reshape(MLA_RANK, -1))
    wv = bf(wkvb[:, :, MLA_NOPE:].reshape(MLA_RANK, -1))
    q, k, v = _mla_in(h, win, wqb, wkn, wv, mla_q_norm[0], mla_kv_norm[0],
                      _rope_tables(s_len, c_len, MLA_ROPE))
    y = _mla_attention(q, k, v, s_len, c_len)
    xl = _outproj([y], bf(w_out_o[0]), xs, norm_g[1, 1], mods[1], 2, s_len, n_lat, b_sz)
    return _moe_block(xl, norm_g[1], mods[1], n_lat, b_sz, wr_t, b_router, *moe(1))
```

```python
import functools

import numpy as np
import jax
import jax.numpy as jnp
from jax import lax
from jax.experimental import pallas as pl
from jax.experimental.pallas import tpu as pltpu

F32 = jnp.float32
BF16 = jnp.bfloat16
I32 = jnp.int32

D = 2048
GRID_W = 64
HD = 128
EPS = 1e-6
NEG = -1e30
ROPE_THETA = 10000.0
LOG2E = 1.4426950408889634

NA_HEADS = 8
NA_WIN_ROWS = 8
NA_WIN_COLS = 16
NA_QROWS = 4
NA_KROWS = NA_QROWS + NA_WIN_ROWS
WG_HEADS = 8
WG_KV = 2
WG_GRP = WG_HEADS // WG_KV
WG_WINDOW = 128
WG_TQ = 256
WG_SPAN = WG_TQ + 2 * WG_WINDOW
EVEN_IN = (3 * NA_HEADS + WG_HEADS + 2 * WG_KV) * HD
MLA_HEADS = 16
MLA_RANK = 512
MLA_NOPE = 128
MLA_ROPE = 64
MLA_V = 128
MLA_QK = 256
MLA_TQ = 512
MLA_TK = 512
N_EXPERTS = 16
N_GROUPS = 4
GROUP_SZ = N_EXPERTS // N_GROUPS
D_EXPERT = 512
D_SHARED = 1024

TM = 256
TMG = 256
VMEM_LIMIT = 56 * 1024 * 1024


def _params(sem, vmem=VMEM_LIMIT):
    return pltpu.CompilerParams(dimension_semantics=sem, vmem_limit_bytes=vmem)


def _dot(a, b):
    return jnp.dot(a, b, preferred_element_type=F32)


def _dot_nt(a, b):
    return lax.dot_general(a, b, (((1,), (1,)), ((), ())), preferred_element_type=F32)


def _resident(shape):
    nd = len(shape)
    return pl.BlockSpec(shape, lambda *_: (0,) * nd, pipeline_mode=pl.Buffered(1))


def _silu(a):
    return a * jax.nn.sigmoid(a)


def _rms(x):
    return x * lax.rsqrt(jnp.mean(x * x, axis=-1, keepdims=True) + EPS)


def _ada_kernel(c_ref, w_ref, b_ref, o_ref):
    s = _silu(c_ref[...]).astype(BF16)
    o_ref[...] = _dot(s, w_ref[...].astype(BF16)) + b_ref[...]


def _ada_mods(cc, ada_w, ada_b):
    n_layers, _, n_out = ada_w.shape
    tn = 1024
    out = pl.pallas_call(
        _ada_kernel,
        name="ada_mods",
        grid=(n_layers, n_out // tn),
        in_specs=[
            pl.BlockSpec((8, D), lambda l, j: (0, 0)),
            pl.BlockSpec((None, D, tn), lambda l, j: (l, 0, j)),
            pl.BlockSpec((None, 1, tn), lambda l, j: (l, 0, j)),
        ],
        out_specs=pl.BlockSpec((None, 8, tn), lambda l, j: (l, 0, j)),
        out_shape=jax.ShapeDtypeStruct((n_layers, 8, n_out), F32),
        compiler_params=_params(("parallel", "parallel")),
    )(cc, ada_w, ada_b.reshape(n_layers, 1, n_out))
    return out.reshape(n_layers, 8, 6, 1, D)


def _mod_spec(which, n_lat_tiles, ctx_row):
    return pl.BlockSpec(
        (None, None, 1, D),
        lambda b, t: (jnp.where(t >= n_lat_tiles, ctx_row, b), which, 0, 0),
    )


def _norm_mod(x, g, sc, sh):
    return (_rms(x) * g) * (1.0 + sc) + sh


def _norm_first_kernel(x_ref, c_ref, g_ref, sc_ref, sh_ref, xs_ref, h_ref, *, n_lat):
    t = pl.program_id(1)

    def emit(src):
        x = src[...]
        xs_ref[...] = x
        h_ref[...] = _norm_mod(x, g_ref[...], sc_ref[...], sh_ref[...]).astype(BF16)

    @pl.when(t < n_lat)
    def _():
        emit(x_ref)

    @pl.when(t == n_lat)
    def _():
        emit(c_ref)


def _norm_first(x, ctx, gain, mods, n_batch):
    b_sz, s_len, _ = x.shape
    c_len = ctx.shape[1]
    n_lat = s_len // TM
    t_len = s_len + c_len
    row = lambda b, t: (b, t, 0)
    return pl.pallas_call(
        functools.partial(_norm_first_kernel, n_lat=n_lat),
        name="norm_first",
        grid=(b_sz, n_lat + 1),
        in_specs=[
            pl.BlockSpec((None, TM, D), lambda b, t: (b, jnp.minimum(t, n_lat - 1), 0)),
            pl.BlockSpec((None, TM, D), lambda b, t: (b, 0, 0)),
            pl.BlockSpec((1, D), lambda b, t: (0, 0)),
            _mod_spec(1, n_lat, n_batch),
            _mod_spec(0, n_lat, n_batch),
        ],
        out_specs=[pl.BlockSpec((None, TM, D), row), pl.BlockSpec((None, TM, D), row)],
        out_shape=[
            jax.ShapeDtypeStruct((b_sz, t_len, D), F32),
            jax.ShapeDtypeStruct((b_sz, t_len, D), BF16),
        ],
        compiler_params=_params(("parallel", "arbitrary")),
    )(x, ctx, gain.reshape(1, D), mods, mods)


def _norm_kernel(x_ref, g_ref, sc_ref, sh_ref, *o_refs):
    h = _norm_mod(x_ref[...], g_ref[...], sc_ref[...], sh_ref[...])
    for o_ref in o_refs:
        o_ref[...] = h.astype(o_ref.dtype)


def _norm(xs, gain, mods, sc_idx, sh_idx, n_lat, n_batch, out_dtypes):
    b_sz, t_len, _ = xs.shape
    row = lambda b, t: (b, t, 0)
    return pl.pallas_call(
        _norm_kernel,
        name="norm_mod",
        grid=(b_sz, t_len // TM),
        in_specs=[
            pl.BlockSpec((None, TM, D), row),
            pl.BlockSpec((1, D), lambda b, t: (0, 0)),
            _mod_spec(sc_idx, n_lat, n_batch),
            _mod_spec(sh_idx, n_lat, n_batch),
        ],
        out_specs=[pl.BlockSpec((None, TM, D), row) for _ in out_dtypes],
        out_shape=[jax.ShapeDtypeStruct((b_sz, t_len, D), dt) for dt in out_dtypes],
        compiler_params=_params(("parallel", "parallel")),
    )(xs, gain.reshape(1, D), mods, mods)


def _rope_tables(s_len, c_len, rot_dim):
    quarter = rot_dim // 4
    t = jnp.arange(s_len, dtype=I32)
    row = (t // GRID_W).astype(F32)
    col = (t % GRID_W).astype(F32)
    inv_freq = ROPE_THETA ** (-jnp.arange(quarter, dtype=F32) / quarter)
    ang_r = row[:, None] * inv_freq[None, :]
    ang_c = col[:, None] * inv_freq[None, :]
    ang = jnp.concatenate([ang_r, ang_r, ang_c, ang_c], axis=-1)
    cos, sin = jnp.cos(ang), jnp.sin(ang)
    even = ((jnp.arange(rot_dim) // quarter) % 2 == 0)[None, :]
    sa = jnp.where(even, -sin, 0.0)
    sb = jnp.where(even, 0.0, sin)
    pad = ((0, c_len), (0, HD - rot_dim))
    return (
        jnp.pad(cos, pad, constant_values=1.0),
        jnp.pad(sa, pad),
        jnp.pad(sb, pad),
    )


def _rope(x, cos, sa, sb, quarter):
    return x * cos + pltpu.roll(x, HD - quarter, 1) * sa + pltpu.roll(x, quarter, 1) * sb


EVEN_TN = 1536
EVEN_ROPE_BLOCK = 2
EVEN_ROPE_CHUNKS = WG_HEADS + WG_KV


def _inproj_even_kernel(h_ref, w_ref, cos_ref, sa_ref, sb_ref, o_ref):
    j = pl.program_id(0)
    acc = _dot(h_ref[...], w_ref[...])

    @pl.when(j != EVEN_ROPE_BLOCK)
    def _():
        o_ref[...] = acc.astype(BF16)

    @pl.when(j == EVEN_ROPE_BLOCK)
    def _():
        cos, sa, sb = cos_ref[...], sa_ref[...], sb_ref[...]
        for ch in range(EVEN_ROPE_CHUNKS):
            sl = slice(ch * HD, (ch + 1) * HD)
            o_ref[:, sl] = _rope(acc[:, sl], cos, sa, sb, HD // 4).astype(BF16)
        rest = slice(EVEN_ROPE_CHUNKS * HD, EVEN_TN)
        o_ref[:, rest] = acc[:, rest].astype(BF16)


def _inproj_even(h, w, tables):
    b_sz, t_len, _ = h.shape
    n_t = t_len // TM
    rows = b_sz * t_len
    tab = pl.BlockSpec((TM, HD), lambda j, m: (m % n_t, 0))
    out = pl.pallas_call(
        _inproj_even_kernel,
        name="inproj_even",
        grid=(EVEN_IN // EVEN_TN, rows // TM),
        in_specs=[
            pl.BlockSpec((TM, D), lambda j, m: (m, 0)),
            pl.BlockSpec((D, EVEN_TN), lambda j, m: (0, j)),
            tab, tab, tab,
        ],
        out_specs=pl.BlockSpec((TM, EVEN_TN), lambda j, m: (m, j)),
        out_shape=jax.ShapeDtypeStruct((rows, EVEN_IN), BF16),
        compiler_params=_params(("parallel", "parallel")),
    )(h.reshape(rows, D), w, *tables)
    return out.reshape(b_sz, t_len, EVEN_IN)


def _na_bias_tables(rpb, rows):
    n_heads, n_dr, n_dc = rpb.shape
    w = GRID_W
    centre = NA_WIN_COLS - 1
    vec = jnp.pad(rpb.astype(F32), ((0, 0), (0, 0), (w - 1 - centre, w - n_dc + centre + 1)))
    toe = jnp.tile(vec, (1, 1, w))[:, :, :w * (2 * w - 1)].reshape(n_heads, n_dr, w, 2 * w - 1)
    toe = toe[:, :, :, w - 1:]
    qc = np.arange(w)
    c_start = np.clip(qc - NA_WIN_COLS // 2, 0, w - NA_WIN_COLS)
    col_ok = (qc[None, :] >= c_start[:, None]) & (qc[None, :] < c_start[:, None] + NA_WIN_COLS)
    n_tiles = rows // NA_QROWS
    tabs = []
    for i in (0, 1, n_tiles - 1):
        r0 = min(max(NA_QROWS * i - NA_WIN_ROWS // 2, 0), rows - NA_KROWS)
        blocks = []
        for qr in range(NA_QROWS * i, NA_QROWS * (i + 1)):
            r_start = min(max(qr - NA_WIN_ROWS // 2, 0), rows - NA_WIN_ROWS)
            per_kr = []
            for kr in range(r0, r0 + NA_KROWS):
                if r_start <= kr < r_start + NA_WIN_ROWS:
                    per_kr.append(jnp.where(col_ok[None], toe[:, kr - qr + NA_WIN_ROWS - 1], NEG))
                else:
                    per_kr.append(jnp.full((n_heads, w, w), NEG, F32))
            blocks.append(jnp.concatenate(per_kr, axis=-1))
        tabs.append(jnp.concatenate(blocks, axis=1))
    return jnp.stack(tabs, axis=1)


def _softmax_pv(parts, extra_logit=None):
    m = parts[0][0].max(axis=-1, keepdims=True)
    for s, _ in parts[1:]:
        m = jnp.maximum(m, s.max(axis=-1, keepdims=True))
    if extra_logit is not None:
        m = jnp.maximum(m, extra_logit)
    l = None
    o = None
    for s, v in parts:
        p = jnp.exp(s - m)
        ls = p.sum(axis=-1, keepdims=True)
        os_ = _dot(p.astype(BF16), v)
        l = ls if l is None else l + ls
        o = os_ if o is None else o + os_
    if extra_logit is not None:
        l = l + jnp.exp(extra_logit - m)
    return o / l


def _na_kernel(q_ref, k_ref, v_ref, bias_ref, o_ref, *, s_len, c_len):
    rows = s_len // GRID_W
    tq = NA_QROWS * GRID_W
    tk = NA_KROWS * GRID_W
    n_q = s_len // tq
    scale = HD ** -0.5
    kc = k_ref[s_len:s_len + c_len, :]
    vc = v_ref[s_len:s_len + c_len, :]

    def body(i, carry):
        qs = pl.multiple_of(i * tq, tq)
        r0 = jnp.clip(NA_QROWS * i - NA_WIN_ROWS // 2, 0, rows - NA_KROWS)
        ks = pl.multiple_of(r0 * GRID_W, tq)
        typ = jnp.where(i == 0, 0, jnp.where(i == n_q - 1, 2, 1))
        q = q_ref[pl.ds(qs, tq), :]
        s_loc = _dot_nt(q, k_ref[pl.ds(ks, tk), :]) * scale + bias_ref[typ]
        s_ctx = _dot_nt(q, kc) * scale
        o = _softmax_pv([(s_loc, v_ref[pl.ds(ks, tk), :]), (s_ctx, vc)])
        o_ref[pl.ds(qs, tq), :] = o.astype(BF16)
        return carry

    lax.fori_loop(0, n_q, body, 0)
    s_cc = _dot_nt(q_ref[s_len:s_len + c_len, :], kc) * scale
    o_ref[s_len:s_len + c_len, :] = _softmax_pv([(s_cc, vc)]).astype(BF16)


def _na_attention(proj, bias, s_len, c_len):
    b_sz, t_len, _ = proj.shape
    col = lambda off: pl.BlockSpec((None, t_len, HD), lambda b, h: (b, 0, off + h))
    return pl.pallas_call(
        functools.partial(_na_kernel, s_len=s_len, c_len=c_len),
        name="na_attention",
        grid=(b_sz, NA_HEADS),
        in_specs=[
            col(0), col(NA_HEADS), col(2 * NA_HEADS),
            pl.BlockSpec((None,) + bias.shape[1:], lambda b, h: (h, 0, 0, 0)),
        ],
        out_specs=pl.BlockSpec((None, t_len, HD), lambda b, h: (b, 0, h)),
        out_shape=jax.ShapeDtypeStruct((b_sz, t_len, NA_HEADS * HD), BF16),
        compiler_params=_params(("parallel", "parallel")),
    )(proj, proj, proj, bias)


def _wg_mask_tables(s_len):
    n_q = s_len // WG_TQ
    tabs = []
    for i in (0, 1, n_q - 1):
        k0 = min(max(WG_TQ * i - WG_WINDOW, 0), s_len - WG_SPAN)
        rel = (k0 + jnp.arange(WG_SPAN))[None, :] - (WG_TQ * i + jnp.arange(WG_TQ))[:, None]
        tabs.append(jnp.where(jnp.abs(rel) <= WG_WINDOW, 0.0, NEG).astype(F32))
    return jnp.stack(tabs)


def _wg_kernel(sink_ref, q_ref, k_ref, v_ref, mask_ref, o_ref, *, s_len, c_len):
    kvh = pl.program_id(1)
    n_q = s_len // WG_TQ
    scale = HD ** -0.5
    kc = k_ref[s_len:s_len + c_len, :]
    vc = v_ref[s_len:s_len + c_len, :]

    def body(i, carry):
        qs = pl.multiple_of(i * WG_TQ, WG_TQ)
        ks = pl.multiple_of(jnp.clip(WG_TQ * i - WG_WINDOW, 0, s_len - WG_SPAN), WG_WINDOW)
        typ = jnp.where(i == 0, 0, jnp.where(i == n_q - 1, 2, 1))
        kl = k_ref[pl.ds(ks, WG_SPAN), :]
        vl = v_ref[pl.ds(ks, WG_SPAN), :]
        mask = mask_ref[typ]
        for g in range(WG_GRP):
            sl = slice(g * HD, (g + 1) * HD)
            q = q_ref[pl.ds(qs, WG_TQ), sl]
            s_loc = _dot_nt(q, kl) * scale + mask
            s_ctx = _dot_nt(q, kc) * scale
            o = _softmax_pv([(s_loc, vl), (s_ctx, vc)], sink_ref[kvh * WG_GRP + g])
            o_ref[pl.ds(qs, WG_TQ), sl] = o.astype(BF16)
        return carry

    lax.fori_loop(0, n_q, body, 0)
    for g in range(WG_GRP):
        sl = slice(g * HD, (g + 1) * HD)
        s_cc = _dot_nt(q_ref[s_len:s_len + c_len, sl], kc) * scale
        o = _softmax_pv([(s_cc, vc)], sink_ref[kvh * WG_GRP + g])
        o_ref[s_len:s_len + c_len, sl] = o.astype(BF16)


def _wg_attention(proj, sink, mask, s_len, c_len):
    b_sz, t_len, _ = proj.shape
    q_off = 3 * NA_HEADS * HD // (WG_GRP * HD)
    k_off = 3 * NA_HEADS + WG_HEADS
    v_off = k_off + WG_KV
    return pl.pallas_call(
        functools.partial(_wg_kernel, s_len=s_len, c_len=c_len),
        name="wg_attention",
        grid=(b_sz, WG_KV),
        in_specs=[
            pl.BlockSpec(memory_space=pltpu.SMEM),
            pl.BlockSpec((None, t_len, WG_GRP * HD), lambda b, h: (b, 0, q_off + h)),
            pl.BlockSpec((None, t_len, HD), lambda b, h: (b, 0, k_off + h)),
            pl.BlockSpec((None, t_len, HD), lambda b, h: (b, 0, v_off + h)),
            pl.BlockSpec(mask.shape, lambda b, h: (0, 0, 0)),
        ],
        out_specs=pl.BlockSpec((None, t_len, WG_GRP * HD), lambda b, h: (b, 0, h)),
        out_shape=jax.ShapeDtypeStruct((b_sz, t_len, WG_HEADS * HD), BF16),
        compiler_params=_params(("parallel", "parallel")),
    )(sink.astype(F32), proj, proj, proj, mask)


def _outproj_kernel(*refs, n_parts):
    y_refs = refs[:n_parts]
    w_ref, x_ref, gate_ref, gain_ref, o_ref = refs[n_parts:]
    y = None
    off = 0
    for y_ref in y_refs:
        k = y_ref.shape[-1]
        part = _dot(y_ref[...], w_ref[off:off + k, :])
        y = part if y is None else y + part
        off += k
    o_ref[...] = x_ref[...] + gate_ref[...] * (_rms(y) * gain_ref[...])


def _outproj(ys, w, xs, gain, mods, gate_idx, n_rows, n_lat, n_batch):
    b_sz = xs.shape[0]
    row = lambda b, t: (b, t, 0)
    return pl.pallas_call(
        functools.partial(_outproj_kernel, n_parts=len(ys)),
        name="outproj_residual",
        grid=(b_sz, n_rows // TM),
        in_specs=[pl.BlockSpec((None, TM, y.shape[-1]), row) for y in ys] + [
            _resident(w.shape),
            pl.BlockSpec((None, TM, D), row),
            _mod_spec(gate_idx, n_lat, n_batch),
            pl.BlockSpec((1, D), lambda b, t: (0, 0)),
        ],
        out_specs=pl.BlockSpec((None, TM, D), row),
        out_shape=jax.ShapeDtypeStruct((b_sz, n_rows, D), F32),
        compiler_params=_params(("parallel", "parallel")),
    )(*ys, w, xs, mods, gain.reshape(1, D))


def _roll_rows(x, shift):
    return jnp.concatenate([x[-shift:], x[:-shift]], axis=0)


def _mla_in_kernel(h_ref, win_ref, wqbt_ref, wkn_ref, wvt_ref, qg_ref, kvg_ref,
                   cos_ref, sa_ref, sb_ref, cost_ref, sat_ref, sbt_ref, q_ref, k_ref, v_ref):
    quarter = MLA_ROPE // 4
    c = _dot(h_ref[...], win_ref[...])
    cq = c[:, :MLA_RANK]
    ckv = c[:, MLA_RANK:2 * MLA_RANK]
    k_rope = _rope(c[:, 2 * MLA_RANK:], cos_ref[...], sa_ref[...], sb_ref[...], quarter).astype(BF16)
    q_scale = (MLA_NOPE + MLA_ROPE) ** -0.5 * LOG2E
    q_t = _dot_nt(wqbt_ref[...], (_rms(cq) * qg_ref[...]).astype(BF16))
    kvn = (_rms(ckv) * kvg_ref[...]).astype(BF16)
    k_nope = _dot(kvn, wkn_ref[...])
    v_t = _dot_nt(wvt_ref[...], kvn)
    cos_t, sa_t, sb_t = cost_ref[...], sat_ref[...], sbt_ref[...]
    for h in range(MLA_HEADS):
        lo = slice(h * MLA_QK, h * MLA_QK + HD)
        hi = slice(h * MLA_QK + HD, (h + 1) * MLA_QK)
        x = q_t[hi]
        x = x * cos_t + _roll_rows(x, HD - quarter) * sa_t + _roll_rows(x, quarter) * sb_t
        q_ref[lo, :] = (q_t[lo] * q_scale).astype(BF16)
        q_ref[hi, :] = (x * q_scale).astype(BF16)
        k_ref[:, lo] = k_nope[:, h * HD:(h + 1) * HD].astype(BF16)
        k_ref[:, hi] = k_rope
        v_ref[h] = v_t[h * MLA_V:(h + 1) * MLA_V].astype(BF16)


def _mla_in(h, win, wqbt, wkn, wvt, q_gain, kv_gain, tables):
    b_sz, t_len, _ = h.shape
    n_t = t_len // TM
    tab = pl.BlockSpec((TM, HD), lambda b, t: (t, 0))
    tab_t = pl.BlockSpec((HD, TM), lambda b, t: (0, t))
    tables_t = tuple(a.T for a in tables)
    return pl.pallas_call(
        _mla_in_kernel,
        name="mla_in",
        grid=(b_sz, n_t),
        in_specs=[
            pl.BlockSpec((None, TM, D), lambda b, t: (b, t, 0)),
            _resident(win.shape), _resident(wqbt.shape), _resident(wkn.shape),
            _resident(wvt.shape), _resident((1, MLA_RANK)), _resident((1, MLA_RANK)),
            tab, tab, tab, tab_t, tab_t, tab_t,
        ],
        out_specs=[
            pl.BlockSpec((None, MLA_HEADS * MLA_QK, TM), lambda b, t: (b, 0, t)),
            pl.BlockSpec((None, TM, MLA_HEADS * MLA_QK), lambda b, t: (b, t, 0)),
            pl.BlockSpec((None, None, MLA_HEADS, MLA_V, TM), lambda b, t: (b, t, 0, 0, 0)),
        ],
        out_shape=[
            jax.ShapeDtypeStruct((b_sz, MLA_HEADS * MLA_QK, t_len), BF16),
            jax.ShapeDtypeStruct((b_sz, t_len, MLA_HEADS * MLA_QK), BF16),
            jax.ShapeDtypeStruct((b_sz, n_t, MLA_HEADS, MLA_V, TM), BF16),
        ],
        compiler_params=_params(("parallel", "parallel")),
    )(h, win, wqbt, wkn, wvt, q_gain.reshape(1, MLA_RANK), kv_gain.reshape(1, MLA_RANK),
      *tables, *tables_t)


MLA_SUB = 64
MLA_SLOTS = 3


def _mla_attn_kernel(q_ref, k_ref, v_ref, o_ref, s_scr, p_scr, *, n_tiles):
    q_t = q_ref[...]

    def logits(tile, slot):
        ks = pl.multiple_of(tile * TM, TM)
        s = _dot(k_ref[pl.ds(ks, TM), :], q_t)
        s_scr[slot] = s
        return s.max(axis=0, keepdims=True)

    def softmax(s_slot, p_slot, m_tile, m, l):
        m_new = jnp.maximum(m, m_tile)
        alpha = jnp.exp2(m - m_new)
        lsum = jnp.zeros((8, MLA_TQ), F32)
        for r in range(0, TM, MLA_SUB):
            p = jnp.exp2(s_scr[s_slot, r:r + MLA_SUB, :] - m_new)
            for g in range(0, MLA_SUB, 8):
                lsum = lsum + p[g:g + 8]
            p_scr[p_slot, r:r + MLA_SUB, :] = p.astype(BF16)
        return m_new, alpha, alpha * l + lsum.sum(axis=0, keepdims=True)

    m = jnp.full((1, MLA_TQ), -jnp.inf, F32)
    l = jnp.zeros((1, MLA_TQ), F32)
    acc = jnp.zeros((MLA_V, MLA_TQ), F32)
    alpha_prev = None
    m_tile = logits(0, 0)
    for t in range(n_tiles):
        m_next = logits(t + 1, (t + 1) % MLA_SLOTS) if t + 1 < n_tiles else None
        if t > 0:
            pv_prev = _dot(v_ref[t - 1], p_scr[(t - 1) % 2])
        m, alpha, l = softmax(t % MLA_SLOTS, t % 2, m_tile, m, l)
        if t > 0:
            acc = alpha_prev * acc + pv_prev
        alpha_prev, m_tile = alpha, m_next
    acc = alpha_prev * acc + _dot(v_ref[n_tiles - 1], p_scr[(n_tiles - 1) % 2])
    o_ref[...] = (acc / l).T.astype(BF16)


def _mla_attention(q_t, k, v_t, s_len, c_len):
    b_sz, t_len, _ = k.shape
    n_t = t_len // TM
    return pl.pallas_call(
        functools.partial(_mla_attn_kernel, n_tiles=n_t),
        name="mla_attention",
        grid=(b_sz, MLA_HEADS, s_len // MLA_TQ),
        in_specs=[
            pl.BlockSpec((None, MLA_QK, MLA_TQ), lambda b, h, i: (b, h, i)),
            pl.BlockSpec((None, t_len, MLA_QK), lambda b, h, i: (b, 0, h)),
            pl.BlockSpec((None, n_t, None, MLA_V, TM), lambda b, h, i: (b, 0, h, 0, 0)),
        ],
        out_specs=pl.BlockSpec((None, MLA_TQ, MLA_V), lambda b, h, i: (b, i, h)),
        out_shape=jax.ShapeDtypeStruct((b_sz, s_len, MLA_HEADS * MLA_V), BF16),
        scratch_shapes=[pltpu.VMEM((MLA_SLOTS, TM, MLA_TQ), F32), pltpu.VMEM((2, TM, MLA_TQ), BF16)],
        compiler_params=_params(("parallel", "parallel", "parallel")),
    )(q_t, k, v_t)


def _router_kernel(h_ref, wr_ref, br_ref, idx_ref, wt_ref):
    scores = jax.nn.sigmoid(_dot_nt(wr_ref[...], h_ref[...]))
    sel = scores + br_ref[...]
    sel_rows = [sel[e:e + 1, :] for e in range(N_EXPERTS)]
    score_rows = [scores[e:e + 1, :] for e in range(N_EXPERTS)]

    group_scores = []
    for g in range(N_GROUPS):
        a, b, c, d = sel_rows[g * GROUP_SZ:(g + 1) * GROUP_SZ]
        hi1, lo1 = jnp.maximum(a, b), jnp.minimum(a, b)
        hi2, lo2 = jnp.maximum(c, d), jnp.minimum(c, d)
        top1 = jnp.maximum(hi1, hi2)
        top2 = jnp.maximum(jnp.minimum(hi1, hi2), jnp.maximum(lo1, lo2))
        group_scores.append(top1 + top2)
    best = group_scores[0]
    gi = jnp.zeros_like(best, dtype=I32)
    for g in range(1, N_GROUPS):
        upd = group_scores[g] > best
        gi = jnp.where(upd, g, gi)
        best = jnp.where(upd, group_scores[g], best)

    def pick_group(rows_, j):
        out = rows_[j]
        for g in range(1, N_GROUPS):
            out = jnp.where(gi == g, rows_[g * GROUP_SZ + j], out)
        return out

    gsel = [pick_group(sel_rows, j) for j in range(GROUP_SZ)]
    gscore = [pick_group(score_rows, j) for j in range(GROUP_SZ)]

    def argmax_excluding(skip):
        val = jnp.full_like(best, -jnp.inf)
        idx = jnp.zeros_like(gi)
        wt = jnp.zeros_like(best)
        for j in range(GROUP_SZ):
            cand = gsel[j] if skip is None else jnp.where(skip == j, -jnp.inf, gsel[j])
            upd = cand > val
            idx = jnp.where(upd, j, idx)
            wt = jnp.where(upd, gscore[j], wt)
            val = jnp.where(upd, cand, val)
        return idx, wt

    i1, w1 = argmax_excluding(None)
    i2, w2 = argmax_excluding(i1)
    tot = w1 + w2
    idx_ref[0:1, :] = gi * GROUP_SZ + i1
    idx_ref[1:2, :] = gi * GROUP_SZ + i2
    wt_ref[0:1, :] = w1 / tot
    wt_ref[1:2, :] = w2 / tot


def _router(h16, wr_t, br):
    rows = h16.shape[0]
    return pl.pallas_call(
        _router_kernel,
        name="moe_router",
        grid=(rows // TM,),
        in_specs=[
            pl.BlockSpec((TM, D), lambda m: (m, 0)),
            pl.BlockSpec((N_EXPERTS, D), lambda m: (0, 0)),
            pl.BlockSpec((N_EXPERTS, 1), lambda m: (0, 0)),
        ],
        out_specs=[pl.BlockSpec((2, TM), lambda m: (0, m)), pl.BlockSpec((2, TM), lambda m: (0, m))],
        out_shape=[jax.ShapeDtypeStruct((2, rows), I32), jax.ShapeDtypeStruct((2, rows), F32)],
        compiler_params=_params(("parallel",)),
    )(h16, wr_t, br.reshape(N_EXPERTS, 1).astype(F32))


def _dispatch_plan(idx, rows):
    n_tiles = 2 * rows // TMG + N_EXPERTS
    e = idx.reshape(-1)
    onehot = (e[:, None] == jnp.arange(N_EXPERTS, dtype=I32)[None, :]).astype(I32)
    csum = jnp.cumsum(onehot, axis=0)
    rank = jnp.take_along_axis(csum, e[:, None], axis=1)[:, 0] - 1
    counts = csum[-1]
    padded = (counts + TMG - 1) // TMG * TMG
    ends = jnp.cumsum(padded)
    dest = (ends - padded)[e] + rank
    tok = jnp.tile(jnp.arange(rows, dtype=I32), 2)
    src = jnp.zeros((n_tiles * TMG,), I32).at[dest].set(tok)
    tile_start = jnp.arange(n_tiles, dtype=I32) * TMG
    tile_expert = jnp.minimum(
        jnp.sum((ends[None, :] <= tile_start[:, None]).astype(I32), axis=1), N_EXPERTS - 1)
    tile_valid = (tile_start < ends[-1]).astype(I32)
    return src, dest.reshape(2, rows).astype(I32), tile_expert, tile_valid, n_tiles


def _gmm_kernel(te_ref, tv_ref, src_ref, h_hbm, wg_ref, wu_ref, wd_ref, y_ref, buf, sem):
    del te_ref
    i = pl.program_id(0)
    n = pl.num_programs(0)

    def gather(tile, slot):
        base = tile * TMG

        def body(r, c):
            pltpu.make_async_copy(
                h_hbm.at[pl.ds(src_ref[base + r], 1)], buf.at[slot, pl.ds(r, 1)], sem.at[slot]
            ).start()
            return c

        lax.fori_loop(0, TMG, body, 0, unroll=8)

    @pl.when((i == 0) & (tv_ref[0] > 0))
    def _():
        gather(0, 0)

    nxt = jnp.minimum(i + 1, n - 1)

    @pl.when((i + 1 < n) & (tv_ref[nxt] > 0))
    def _():
        gather(i + 1, (i + 1) % 2)

    slot = i % 2

    @pl.when(tv_ref[i] > 0)
    def _():
        pltpu.make_async_copy(h_hbm.at[pl.ds(0, TMG)], buf.at[slot], sem.at[slot]).wait()
        x = buf[slot].astype(BF16)
        z = _silu(_dot(x, wg_ref[...])) * _dot(x, wu_ref[...])
        y_ref[...] = _dot(z.astype(BF16), wd_ref[...])

    @pl.when(tv_ref[i] == 0)
    def _():
        y_ref[...] = jnp.zeros_like(y_ref)


def _gmm(h32, src, tile_expert, tile_valid, n_tiles, wg, wu, wd):
    expert = lambda i, te, tv, src_: (te[i], 0, 0)
    grid_spec = pltpu.PrefetchScalarGridSpec(
        num_scalar_prefetch=3,
        grid=(n_tiles,),
        in_specs=[
            pl.BlockSpec(memory_space=pl.ANY),
            pl.BlockSpec((None, D, D_EXPERT), expert),
            pl.BlockSpec((None, D, D_EXPERT), expert),
            pl.BlockSpec((None, D_EXPERT, D), expert),
        ],
        out_specs=pl.BlockSpec((TMG, D), lambda i, te, tv, src_: (i, 0)),
        scratch_shapes=[pltpu.VMEM((2, TMG, D), F32), pltpu.SemaphoreType.DMA((2,))],
    )
    return pl.pallas_call(
        _gmm_kernel,
        name="moe_experts",
        grid_spec=grid_spec,
        out_shape=jax.ShapeDtypeStruct((n_tiles * TMG, D), F32),
        compiler_params=_params(("arbitrary",)),
    )(tile_expert, tile_valid, src, h32, wg, wu, wd)


def _ffn_kernel(d0_ref, d1_ref, h_ref, wsg_ref, wsu_ref, wsd_ref, y_hbm, wt_ref, x_ref,
                gate_ref, gain_ref, o_ref, buf, sem, *, n_t):
    base = (pl.program_id(0) * n_t + pl.program_id(1)) * TM

    def body(r, c):
        pltpu.make_async_copy(y_hbm.at[pl.ds(d0_ref[base + r], 1)], buf.at[0, pl.ds(r, 1)], sem.at[0]).start()
        pltpu.make_async_copy(y_hbm.at[pl.ds(d1_ref[base + r], 1)], buf.at[1, pl.ds(r, 1)], sem.at[1]).start()
        return c

    lax.fori_loop(0, TM, body, 0, unroll=8)
    h = h_ref[...]
    z = _silu(_dot(h, wsg_ref[...])) * _dot(h, wsu_ref[...])
    shared = _dot(z.astype(BF16), wsd_ref[...])
    for s in range(2):
        pltpu.make_async_copy(y_hbm.at[pl.ds(0, TM)], buf.at[s], sem.at[s]).wait()
    wt = wt_ref[...]
    fx = wt[:, 0:1] * buf[0] + wt[:, 1:2] * buf[1] + shared
    o_ref[...] = x_ref[...] + gate_ref[...] * (_rms(fx) * gain_ref[...])


def _ffn_combine(h16, y, dest, wts, xs, wsg, wsu, wsd, gain, mods, n_lat, n_batch):
    b_sz, t_len, _ = xs.shape
    n_t = t_len // TM
    row = lambda b, t, d0, d1: (b, t, 0)
    const2 = lambda b, t, d0, d1: (0, 0)
    mod = pl.BlockSpec(
        (None, None, 1, D),
        lambda b, t, d0, d1: (jnp.where(t >= n_lat, n_batch, b), 5, 0, 0),
    )
    grid_spec = pltpu.PrefetchScalarGridSpec(
        num_scalar_prefetch=2,
        grid=(b_sz, n_t),
        in_specs=[
            pl.BlockSpec((None, TM, D), row),
            pl.BlockSpec(wsg.shape, const2, pipeline_mode=pl.Buffered(1)),
            pl.BlockSpec(wsu.shape, const2, pipeline_mode=pl.Buffered(1)),
            pl.BlockSpec(wsd.shape, const2, pipeline_mode=pl.Buffered(1)),
            pl.BlockSpec(memory_space=pl.ANY),
            pl.BlockSpec((None, TM, 2), row),
            pl.BlockSpec((None, TM, D), row),
            mod,
            pl.BlockSpec((1, D), const2),
        ],
        out_specs=pl.BlockSpec((None, TM, D), row),
        scratch_shapes=[pltpu.VMEM((2, TM, D), F32), pltpu.SemaphoreType.DMA((2,))],
    )
    return pl.pallas_call(
        functools.partial(_ffn_kernel, n_t=n_t),
        name="moe_shared_combine",
        grid_spec=grid_spec,
        out_shape=jax.ShapeDtypeStruct((b_sz, t_len, D), F32),
        compiler_params=_params(("arbitrary", "arbitrary")),
    )(dest[0], dest[1], h16, wsg, wsu, wsd, y, wts.T.reshape(b_sz, t_len, 2), xs, mods,
      gain.reshape(1, D))


def _moe_block(xs, gains, mods, n_lat, n_batch, wr_t, br, wg, wu, wd, wsg, wsu, wsd):
    b_sz, t_len, _ = xs.shape
    rows = b_sz * t_len
    h32, h16 = _norm(xs, gains[2], mods, 4, 3, n_lat, n_batch, (F32, BF16))
    idx, wts = _router(h16.reshape(rows, D), wr_t, br)
    src, dest, tile_expert, tile_valid, n_tiles = _dispatch_plan(idx, rows)
    y = _gmm(h32.reshape(rows, D), src, tile_expert, tile_valid, n_tiles, wg, wu, wd)
    return _ffn_combine(h16, y, dest, wts, xs, wsg, wsu, wsd, gains[3], mods, n_lat, n_batch)


def kernel(x, c, ctx, c_ctx, ada_w, ada_b, norm_g, w_in_e, w_out_e, na_rpb, wg_sink, w_in_o, mla_q_norm, mla_kv_norm, mla_w_qb, mla_w_kvb, w_out_o, w_router, b_router, moe_w_gate, moe_w_up, moe_w_down, shared_w_gate, shared_w_up, shared_w_down):
    b_sz, s_len, _ = x.shape
    c_len = ctx.shape[1]
    assert c_len == TM and s_len % MLA_TQ == 0 and b_sz < 8
    n_lat = s_len // TM
    bf = lambda a: a.astype(BF16)

    cc = jnp.zeros((8, D), F32).at[:b_sz].set(c).at[b_sz].set(c_ctx)
    mods = _ada_mods(cc, ada_w, ada_b)
    wr_t = bf(w_router.T)
    moe = lambda layer: (bf(moe_w_gate[layer]), bf(moe_w_up[layer]), bf(moe_w_down[layer]),
                         bf(shared_w_gate[layer]), bf(shared_w_up[layer]), bf(shared_w_down[layer]))

    xs, h = _norm_first(x, ctx, norm_g[0, 0], mods[0], b_sz)
    proj = _inproj_even(h, bf(w_in_e[0]), _rope_tables(s_len, c_len, HD))
    ya = _na_attention(proj, _na_bias_tables(na_rpb[0], s_len // GRID_W), s_len, c_len)
    yw = _wg_attention(proj, wg_sink[0], _wg_mask_tables(s_len), s_len, c_len)
    xs = _outproj([ya, yw], bf(w_out_e[0]), xs, norm_g[0, 1], mods[0], 2, s_len + c_len, n_lat, b_sz)
    xs = _moe_block(xs, norm_g[0], mods[0], n_lat, b_sz, wr_t, b_router, *moe(0))

    (h,) = _norm(xs, norm_g[1, 0], mods[1], 1, 0, n_lat, b_sz, (BF16,))
    win = bf(jnp.pad(w_in_o[0], ((0, 0), (0, HD - MLA_ROPE))))
    wqb = bf(jnp.pad(mla_w_qb[0].reshape(MLA_RANK, MLA_HEADS, MLA_NOPE + MLA_ROPE),
                     ((0, 0), (0, 0), (0, MLA_QK - MLA_NOPE - MLA_ROPE))).reshape(MLA_RANK, -1))
    wkvb = mla_w_kvb[0].reshape(MLA_RANK, MLA_HEADS, MLA_NOPE + MLA_V)
    wkn = bf(wkvb[:, :, :MLA_NOPE].reshape(MLA_RANK, -1))
    wv = bf(wkvb[:, :, MLA_NOPE:].reshape(MLA_RANK, -1))
    q_t, k, v_t = _mla_in(h, win, wqb.T, wkn, wv.T, mla_q_norm[0], mla_kv_norm[0],
                          _rope_tables(s_len, c_len, MLA_ROPE))
    y = _mla_attention(q_t, k, v_t, s_len, c_len)
    xl = _outproj([y], bf(w_out_o[0]), xs, norm_g[1, 1], mods[1], 2, s_len, n_lat, b_sz)
    return _moe_block(xl, norm_g[1], mods[1], n_lat, b_sz, wr_t, b_router, *moe(1))
```

```python
import functools

import numpy as np
import jax
import jax.numpy as jnp
from jax import lax
from jax.experimental import pallas as pl
from jax.experimental.pallas import tpu as pltpu

F32 = jnp.float32
BF16 = jnp.bfloat16
I32 = jnp.int32

D = 2048
GRID_W = 64
HD = 128
EPS = 1e-6
NEG = -1e30
ROPE_THETA = 10000.0
LOG2E = 1.4426950408889634

NA_HEADS = 8
NA_WIN_ROWS = 8
NA_WIN_COLS = 16
NA_QROWS = 4
NA_KROWS = NA_QROWS + NA_WIN_ROWS
WG_HEADS = 8
WG_KV = 2
WG_GRP = WG_HEADS // WG_KV
WG_WINDOW = 128
WG_TQ = 256
WG_SPAN = WG_TQ + 2 * WG_WINDOW
EVEN_IN = (3 * NA_HEADS + WG_HEADS + 2 * WG_KV) * HD
MLA_HEADS = 16
MLA_RANK = 512
MLA_NOPE = 128
MLA_ROPE = 64
MLA_V = 128
MLA_QK = 256
MLA_TQ = 512
MLA_TK = 512
N_EXPERTS = 16
N_GROUPS = 4
GROUP_SZ = N_EXPERTS // N_GROUPS
D_EXPERT = 512
D_SHARED = 1024

TM = 256
TMG = 256
VMEM_LIMIT = 56 * 1024 * 1024


def _params(sem, vmem=VMEM_LIMIT):
    return pltpu.CompilerParams(dimension_semantics=sem, vmem_limit_bytes=vmem)


def _dot(a, b):
    return jnp.dot(a, b, preferred_element_type=F32)


def _dot_nt(a, b):
    return lax.dot_general(a, b, (((1,), (1,)), ((), ())), preferred_element_type=F32)


def _resident(shape):
    nd = len(shape)
    return pl.BlockSpec(shape, lambda *_: (0,) * nd, pipeline_mode=pl.Buffered(1))


def _silu(a):
    return a * jax.nn.sigmoid(a)


def _rms(x):
    return x * lax.rsqrt(jnp.mean(x * x, axis=-1, keepdims=True) + EPS)


def _ada_kernel(c_ref, w_ref, b_ref, o_ref):
    s = _silu(c_ref[...]).astype(BF16)
    o_ref[...] = _dot(s, w_ref[...].astype(BF16)) + b_ref[...]


def _ada_mods(cc, ada_w, ada_b):
    n_layers, _, n_out = ada_w.shape
    tn = 1024
    out = pl.pallas_call(
        _ada_kernel,
        name="ada_mods",
        grid=(n_layers, n_out // tn),
        in_specs=[
            pl.BlockSpec((8, D), lambda l, j: (0, 0)),
            pl.BlockSpec((None, D, tn), lambda l, j: (l, 0, j)),
            pl.BlockSpec((None, 1, tn), lambda l, j: (l, 0, j)),
        ],
        out_specs=pl.BlockSpec((None, 8, tn), lambda l, j: (l, 0, j)),
        out_shape=jax.ShapeDtypeStruct((n_layers, 8, n_out), F32),
        compiler_params=_params(("parallel", "parallel")),
    )(cc, ada_w, ada_b.reshape(n_layers, 1, n_out))
    return out.reshape(n_layers, 8, 6, 1, D)


def _mod_spec(which, n_lat_tiles, ctx_row):
    return pl.BlockSpec(
        (None, None, 1, D),
        lambda b, t: (jnp.where(t >= n_lat_tiles, ctx_row, b), which, 0, 0),
    )


def _norm_mod(x, g, sc, sh):
    return (_rms(x) * g) * (1.0 + sc) + sh


def _norm_first_kernel(x_ref, c_ref, g_ref, sc_ref, sh_ref, xs_ref, h_ref, *, n_lat):
    t = pl.program_id(1)

    def emit(src):
        x = src[...]
        xs_ref[...] = x
        h_ref[...] = _norm_mod(x, g_ref[...], sc_ref[...], sh_ref[...]).astype(BF16)

    @pl.when(t < n_lat)
    def _():
        emit(x_ref)

    @pl.when(t == n_lat)
    def _():
        emit(c_ref)


def _norm_first(x, ctx, gain, mods, n_batch):
    b_sz, s_len, _ = x.shape
    c_len = ctx.shape[1]
    n_lat = s_len // TM
    t_len = s_len + c_len
    row = lambda b, t: (b, t, 0)
    return pl.pallas_call(
        functools.partial(_norm_first_kernel, n_lat=n_lat),
        name="norm_first",
        grid=(b_sz, n_lat + 1),
        in_specs=[
            pl.BlockSpec((None, TM, D), lambda b, t: (b, jnp.minimum(t, n_lat - 1), 0)),
            pl.BlockSpec((None, TM, D), lambda b, t: (b, 0, 0)),
            pl.BlockSpec((1, D), lambda b, t: (0, 0)),
            _mod_spec(1, n_lat, n_batch),
            _mod_spec(0, n_lat, n_batch),
        ],
        out_specs=[pl.BlockSpec((None, TM, D), row), pl.BlockSpec((None, TM, D), row)],
        out_shape=[
            jax.ShapeDtypeStruct((b_sz, t_len, D), F32),
            jax.ShapeDtypeStruct((b_sz, t_len, D), BF16),
        ],
        compiler_params=_params(("parallel", "arbitrary")),
    )(x, ctx, gain.reshape(1, D), mods, mods)


def _rope_tables(s_len, c_len, rot_dim):
    quarter = rot_dim // 4
    t = jnp.arange(s_len, dtype=I32)
    row = (t // GRID_W).astype(F32)
    col = (t % GRID_W).astype(F32)
    inv_freq = ROPE_THETA ** (-jnp.arange(quarter, dtype=F32) / quarter)
    ang_r = row[:, None] * inv_freq[None, :]
    ang_c = col[:, None] * inv_freq[None, :]
    ang = jnp.concatenate([ang_r, ang_r, ang_c, ang_c], axis=-1)
    cos, sin = jnp.cos(ang), jnp.sin(ang)
    even = ((jnp.arange(rot_dim) // quarter) % 2 == 0)[None, :]
    sa = jnp.where(even, -sin, 0.0)
    sb = jnp.where(even, 0.0, sin)
    pad = ((0, c_len), (0, HD - rot_dim))
    return (
        jnp.pad(cos, pad, constant_values=1.0),
        jnp.pad(sa, pad),
        jnp.pad(sb, pad),
    )


def _rope(x, cos, sa, sb, quarter):
    return x * cos + pltpu.roll(x, HD - quarter, 1) * sa + pltpu.roll(x, quarter, 1) * sb


EVEN_TN = 1536
EVEN_ROPE_BLOCK = 2
EVEN_ROPE_CHUNKS = WG_HEADS + WG_KV


ATTN_QSCALE = HD ** -0.5 * LOG2E


def _inproj_even_kernel(h_ref, w_ref, cos_ref, sa_ref, sb_ref, o_ref):
    j = pl.program_id(0)
    acc = _dot(h_ref[...], w_ref[...])
    n_q = NA_HEADS * HD

    @pl.when(j == 0)
    def _():
        o_ref[:, :n_q] = (acc[:, :n_q] * ATTN_QSCALE).astype(BF16)
        o_ref[:, n_q:] = acc[:, n_q:].astype(BF16)

    @pl.when((j != 0) & (j != EVEN_ROPE_BLOCK))
    def _():
        o_ref[...] = acc.astype(BF16)

    @pl.when(j == EVEN_ROPE_BLOCK)
    def _():
        cos, sa, sb = cos_ref[...], sa_ref[...], sb_ref[...]
        for ch in range(EVEN_ROPE_CHUNKS):
            sl = slice(ch * HD, (ch + 1) * HD)
            r = _rope(acc[:, sl], cos, sa, sb, HD // 4)
            o_ref[:, sl] = (r * ATTN_QSCALE if ch < WG_HEADS else r).astype(BF16)
        rest = slice(EVEN_ROPE_CHUNKS * HD, EVEN_TN)
        o_ref[:, rest] = acc[:, rest].astype(BF16)


def _inproj_even(h, w, tables):
    b_sz, t_len, _ = h.shape
    n_t = t_len // TM
    rows = b_sz * t_len
    tab = pl.BlockSpec((TM, HD), lambda j, m: (m % n_t, 0))
    out = pl.pallas_call(
        _inproj_even_kernel,
        name="inproj_even",
        grid=(EVEN_IN // EVEN_TN, rows // TM),
        in_specs=[
            pl.BlockSpec((TM, D), lambda j, m: (m, 0)),
            pl.BlockSpec((D, EVEN_TN), lambda j, m: (0, j)),
            tab, tab, tab,
        ],
        out_specs=pl.BlockSpec((TM, EVEN_TN), lambda j, m: (m, j)),
        out_shape=jax.ShapeDtypeStruct((rows, EVEN_IN), BF16),
        compiler_params=_params(("parallel", "parallel")),
    )(h.reshape(rows, D), w, *tables)
    return out.reshape(b_sz, t_len, EVEN_IN)


def _na_bias_tables(rpb, rows):
    n_heads, n_dr, n_dc = rpb.shape
    w = GRID_W
    centre = NA_WIN_COLS - 1
    vec = jnp.pad(rpb.astype(F32) * LOG2E, ((0, 0), (0, 0), (w - 1 - centre, w - n_dc + centre + 1)))
    toe = jnp.tile(vec, (1, 1, w))[:, :, :w * (2 * w - 1)].reshape(n_heads, n_dr, w, 2 * w - 1)
    toe = toe[:, :, :, w - 1:]
    qc = np.arange(w)
    c_start = np.clip(qc - NA_WIN_COLS // 2, 0, w - NA_WIN_COLS)
    col_ok = (qc[None, :] >= c_start[:, None]) & (qc[None, :] < c_start[:, None] + NA_WIN_COLS)
    n_tiles = rows // NA_QROWS
    tabs = []
    for i in (0, 1, n_tiles - 1):
        r0 = min(max(NA_QROWS * i - NA_WIN_ROWS // 2, 0), rows - NA_KROWS)
        blocks = []
        for qr in range(NA_QROWS * i, NA_QROWS * (i + 1)):
            r_start = min(max(qr - NA_WIN_ROWS // 2, 0), rows - NA_WIN_ROWS)
            per_kr = []
            for kr in range(r0, r0 + NA_KROWS):
                if r_start <= kr < r_start + NA_WIN_ROWS:
                    per_kr.append(jnp.where(col_ok[None], toe[:, kr - qr + NA_WIN_ROWS - 1], NEG))
                else:
                    per_kr.append(jnp.full((n_heads, w, w), NEG, F32))
            blocks.append(jnp.concatenate(per_kr, axis=-1))
        tabs.append(jnp.concatenate(blocks, axis=1))
    return jnp.stack(tabs, axis=1)


def _softmax_pv(parts, extra_logit=None):
    m = parts[0][0].max(axis=-1, keepdims=True)
    for s, _ in parts[1:]:
        m = jnp.maximum(m, s.max(axis=-1, keepdims=True))
    if extra_logit is not None:
        m = jnp.maximum(m, extra_logit)
    l = None
    o = None
    for s, v in parts:
        p = jnp.exp2(s - m)
        ls = p.sum(axis=-1, keepdims=True)
        os_ = _dot(p.astype(BF16), v)
        l = ls if l is None else l + ls
        o = os_ if o is None else o + os_
    if extra_logit is not None:
        l = l + jnp.exp2(extra_logit - m)
    return o / l


def _na_kernel(q_ref, k_ref, v_ref, bias_ref, o_ref, *, s_len, c_len):
    rows = s_len // GRID_W
    tq = NA_QROWS * GRID_W
    tk = NA_KROWS * GRID_W
    n_q = s_len // tq
    kc = k_ref[s_len:s_len + c_len, :]
    vc = v_ref[s_len:s_len + c_len, :]

    def body(i, carry):
        qs = pl.multiple_of(i * tq, tq)
        r0 = jnp.clip(NA_QROWS * i - NA_WIN_ROWS // 2, 0, rows - NA_KROWS)
        ks = pl.multiple_of(r0 * GRID_W, tq)
        typ = jnp.where(i == 0, 0, jnp.where(i == n_q - 1, 2, 1))
        q = q_ref[pl.ds(qs, tq), :]
        s_loc = _dot_nt(q, k_ref[pl.ds(ks, tk), :]) + bias_ref[typ]
        s_ctx = _dot_nt(q, kc)
        o = _softmax_pv([(s_loc, v_ref[pl.ds(ks, tk), :]), (s_ctx, vc)])
        o_ref[pl.ds(qs, tq), :] = o.astype(BF16)
        return carry

    lax.fori_loop(0, n_q, body, 0)
    s_cc = _dot_nt(q_ref[s_len:s_len + c_len, :], kc)
    o_ref[s_len:s_len + c_len, :] = _softmax_pv([(s_cc, vc)]).astype(BF16)


def _na_attention(proj, bias, s_len, c_len):
    b_sz, t_len, _ = proj.shape
    col = lambda off: pl.BlockSpec((None, t_len, HD), lambda b, h: (b, 0, off + h))
    return pl.pallas_call(
        functools.partial(_na_kernel, s_len=s_len, c_len=c_len),
        name="na_attention",
        grid=(b_sz, NA_HEADS),
        in_specs=[
            col(0), col(NA_HEADS), col(2 * NA_HEADS),
            pl.BlockSpec((None,) + bias.shape[1:], lambda b, h: (h, 0, 0, 0)),
        ],
        out_specs=pl.BlockSpec((None, t_len, HD), lambda b, h: (b, 0, h)),
        out_shape=jax.ShapeDtypeStruct((b_sz, t_len, NA_HEADS * HD), BF16),
        compiler_params=_params(("parallel", "parallel")),
    )(proj, proj, proj, bias)


def _wg_mask_tables(s_len):
    n_q = s_len // WG_TQ
    tabs = []
    for i in (0, 1, n_q - 1):
        k0 = min(max(WG_TQ * i - WG_WINDOW, 0), s_len - WG_SPAN)
        rel = (k0 + jnp.arange(WG_SPAN))[None, :] - (WG_TQ * i + jnp.arange(WG_TQ))[:, None]
        tabs.append(jnp.where(jnp.abs(rel) <= WG_WINDOW, 0.0, NEG).astype(F32))
    return jnp.stack(tabs)


def _wg_kernel(sink_ref, q_ref, k_ref, v_ref, mask_ref, o_ref, *, s_len, c_len):
    kvh = pl.program_id(1)
    n_q = s_len // WG_TQ
    kc = k_ref[s_len:s_len + c_len, :]
    vc = v_ref[s_len:s_len + c_len, :]
    sinks = [sink_ref[kvh * WG_GRP + g] * LOG2E for g in range(WG_GRP)]

    def body(i, carry):
        qs = pl.multiple_of(i * WG_TQ, WG_TQ)
        ks = pl.multiple_of(jnp.clip(WG_TQ * i - WG_WINDOW, 0, s_len - WG_SPAN), WG_WINDOW)
        typ = jnp.where(i == 0, 0, jnp.where(i == n_q - 1, 2, 1))
        kl = k_ref[pl.ds(ks, WG_SPAN), :]
        vl = v_ref[pl.ds(ks, WG_SPAN), :]
        mask = mask_ref[typ]
        for g in range(WG_GRP):
            sl = slice(g * HD, (g + 1) * HD)
            q = q_ref[pl.ds(qs, WG_TQ), sl]
            o = _softmax_pv([(_dot_nt(q, kl) + mask, vl), (_dot_nt(q, kc), vc)], sinks[g])
            o_ref[pl.ds(qs, WG_TQ), sl] = o.astype(BF16)
        return carry

    lax.fori_loop(0, n_q, body, 0)
    for g in range(WG_GRP):
        sl = slice(g * HD, (g + 1) * HD)
        o = _softmax_pv([(_dot_nt(q_ref[s_len:s_len + c_len, sl], kc), vc)], sinks[g])
        o_ref[s_len:s_len + c_len, sl] = o.astype(BF16)


def _wg_attention(proj, sink, mask, s_len, c_len):
    b_sz, t_len, _ = proj.shape
    q_off = 3 * NA_HEADS * HD // (WG_GRP * HD)
    k_off = 3 * NA_HEADS + WG_HEADS
    v_off = k_off + WG_KV
    return pl.pallas_call(
        functools.partial(_wg_kernel, s_len=s_len, c_len=c_len),
        name="wg_attention",
        grid=(b_sz, WG_KV),
        in_specs=[
            pl.BlockSpec(memory_space=pltpu.SMEM),
            pl.BlockSpec((None, t_len, WG_GRP * HD), lambda b, h: (b, 0, q_off + h)),
            pl.BlockSpec((None, t_len, HD), lambda b, h: (b, 0, k_off + h)),
            pl.BlockSpec((None, t_len, HD), lambda b, h: (b, 0, v_off + h)),
            pl.BlockSpec(mask.shape, lambda b, h: (0, 0, 0)),
        ],
        out_specs=pl.BlockSpec((None, t_len, WG_GRP * HD), lambda b, h: (b, 0, h)),
        out_shape=jax.ShapeDtypeStruct((b_sz, t_len, WG_HEADS * HD), BF16),
        compiler_params=_params(("parallel", "parallel")),
    )(sink.astype(F32), proj, proj, proj, mask)


def _outproj_kernel(*refs, n_parts):
    y_refs = refs[:n_parts]
    (w_ref, x_ref, gate_ref, gain_ref, ngain_ref, nsc_ref, nsh_ref,
     o_ref, h32_ref, h16_ref) = refs[n_parts:]
    y = None
    off = 0
    for y_ref in y_refs:
        k = y_ref.shape[-1]
        part = _dot(y_ref[...], w_ref[off:off + k, :])
        y = part if y is None else y + part
        off += k
    x = x_ref[...] + gate_ref[...] * (_rms(y) * gain_ref[...])
    o_ref[...] = x
    h = _norm_mod(x, ngain_ref[...], nsc_ref[...], nsh_ref[...])
    h32_ref[...] = h
    h16_ref[...] = h.astype(BF16)


def _outproj(ys, w, xs, gains, mods, n_rows, n_lat, n_batch):
    b_sz = xs.shape[0]
    row = lambda b, t: (b, t, 0)
    vec = pl.BlockSpec((1, D), lambda b, t: (0, 0))
    tile = pl.BlockSpec((None, TM, D), row)
    return pl.pallas_call(
        functools.partial(_outproj_kernel, n_parts=len(ys)),
        name="outproj_residual",
        grid=(b_sz, n_rows // TM),
        in_specs=[pl.BlockSpec((None, TM, y.shape[-1]), row) for y in ys] + [
            _resident(w.shape), tile, _mod_spec(2, n_lat, n_batch), vec,
            vec, _mod_spec(4, n_lat, n_batch), _mod_spec(3, n_lat, n_batch),
        ],
        out_specs=[tile, tile, tile],
        out_shape=[
            jax.ShapeDtypeStruct((b_sz, n_rows, D), F32),
            jax.ShapeDtypeStruct((b_sz, n_rows, D), F32),
            jax.ShapeDtypeStruct((b_sz, n_rows, D), BF16),
        ],
        compiler_params=_params(("parallel", "parallel")),
    )(*ys, w, xs, mods, gains[1].reshape(1, D), gains[2].reshape(1, D), mods, mods)


def _roll_rows(x, shift):
    return jnp.concatenate([x[-shift:], x[:-shift]], axis=0)


def _mla_in_kernel(h_ref, win_ref, wqbt_ref, wkn_ref, wvt_ref, qg_ref, kvg_ref,
                   cos_ref, sa_ref, sb_ref, cost_ref, sat_ref, sbt_ref, q_ref, k_ref, v_ref):
    quarter = MLA_ROPE // 4
    c = _dot(h_ref[...], win_ref[...])
    cq = c[:, :MLA_RANK]
    ckv = c[:, MLA_RANK:2 * MLA_RANK]
    k_rope = _rope(c[:, 2 * MLA_RANK:], cos_ref[...], sa_ref[...], sb_ref[...], quarter).astype(BF16)
    q_scale = (MLA_NOPE + MLA_ROPE) ** -0.5 * LOG2E
    q_t = _dot_nt(wqbt_ref[...], (_rms(cq) * qg_ref[...]).astype(BF16))
    kvn = (_rms(ckv) * kvg_ref[...]).astype(BF16)
    k_nope = _dot(kvn, wkn_ref[...])
    v_t = _dot_nt(wvt_ref[...], kvn)
    cos_t, sa_t, sb_t = cost_ref[...], sat_ref[...], sbt_ref[...]
    for h in range(MLA_HEADS):
        lo = slice(h * MLA_QK, h * MLA_QK + HD)
        hi = slice(h * MLA_QK + HD, (h + 1) * MLA_QK)
        x = q_t[hi]
        x = x * cos_t + _roll_rows(x, HD - quarter) * sa_t + _roll_rows(x, quarter) * sb_t
        q_ref[lo, :] = (q_t[lo] * q_scale).astype(BF16)
        q_ref[hi, :] = (x * q_scale).astype(BF16)
        k_ref[:, lo] = k_nope[:, h * HD:(h + 1) * HD].astype(BF16)
        k_ref[:, hi] = k_rope
        v_ref[h] = v_t[h * MLA_V:(h + 1) * MLA_V].astype(BF16)


def _mla_in(h, win, wqbt, wkn, wvt, q_gain, kv_gain, tables):
    b_sz, t_len, _ = h.shape
    n_t = t_len // TM
    tab = pl.BlockSpec((TM, HD), lambda b, t: (t, 0))
    tab_t = pl.BlockSpec((HD, TM), lambda b, t: (0, t))
    tables_t = tuple(a.T for a in tables)
    return pl.pallas_call(
        _mla_in_kernel,
        name="mla_in",
        grid=(b_sz, n_t),
        in_specs=[
            pl.BlockSpec((None, TM, D), lambda b, t: (b, t, 0)),
            _resident(win.shape), _resident(wqbt.shape), _resident(wkn.shape),
            _resident(wvt.shape), _resident((1, MLA_RANK)), _resident((1, MLA_RANK)),
            tab, tab, tab, tab_t, tab_t, tab_t,
        ],
        out_specs=[
            pl.BlockSpec((None, MLA_HEADS * MLA_QK, TM), lambda b, t: (b, 0, t)),
            pl.BlockSpec((None, TM, MLA_HEADS * MLA_QK), lambda b, t: (b, t, 0)),
            pl.BlockSpec((None, None, MLA_HEADS, MLA_V, TM), lambda b, t: (b, t, 0, 0, 0)),
        ],
        out_shape=[
            jax.ShapeDtypeStruct((b_sz, MLA_HEADS * MLA_QK, t_len), BF16),
            jax.ShapeDtypeStruct((b_sz, t_len, MLA_HEADS * MLA_QK), BF16),
            jax.ShapeDtypeStruct((b_sz, n_t, MLA_HEADS, MLA_V, TM), BF16),
        ],
        compiler_params=_params(("parallel", "parallel")),
    )(h, win, wqbt, wkn, wvt, q_gain.reshape(1, MLA_RANK), kv_gain.reshape(1, MLA_RANK),
      *tables, *tables_t)


MLA_SUB = 64
MLA_SLOTS = 3


def _mla_attn_kernel(q_ref, k_ref, v_ref, o_ref, s_scr, p_scr, *, n_tiles):
    q_t = q_ref[...]

    def logits(tile, slot):
        ks = pl.multiple_of(tile * TM, TM)
        s = _dot(k_ref[pl.ds(ks, TM), :], q_t)
        s_scr[slot] = s
        return s.max(axis=0, keepdims=True)

    def softmax(s_slot, p_slot, m_tile, m, l):
        m_new = jnp.maximum(m, m_tile)
        alpha = jnp.exp2(m - m_new)
        lsum = jnp.zeros((8, MLA_TQ), F32)
        for r in range(0, TM, MLA_SUB):
            p = jnp.exp2(s_scr[s_slot, r:r + MLA_SUB, :] - m_new)
            for g in range(0, MLA_SUB, 8):
                lsum = lsum + p[g:g + 8]
            p_scr[p_slot, r:r + MLA_SUB, :] = p.astype(BF16)
        return m_new, alpha, alpha * l + lsum.sum(axis=0, keepdims=True)

    m = jnp.full((1, MLA_TQ), -jnp.inf, F32)
    l = jnp.zeros((1, MLA_TQ), F32)
    acc = jnp.zeros((MLA_V, MLA_TQ), F32)
    alpha_prev = None
    m_tile = logits(0, 0)
    for t in range(n_tiles):
        m_next = logits(t + 1, (t + 1) % MLA_SLOTS) if t + 1 < n_tiles else None
        if t > 0:
            pv_prev = _dot(v_ref[t - 1], p_scr[(t - 1) % 2])
        m, alpha, l = softmax(t % MLA_SLOTS, t % 2, m_tile, m, l)
        if t > 0:
            acc = alpha_prev * acc + pv_prev
        alpha_prev, m_tile = alpha, m_next
    acc = alpha_prev * acc + _dot(v_ref[n_tiles - 1], p_scr[(n_tiles - 1) % 2])
    o_ref[...] = (acc / l).T.astype(BF16)


def _mla_attention(q_t, k, v_t, s_len, c_len):
    b_sz, t_len, _ = k.shape
    n_t = t_len // TM
    return pl.pallas_call(
        functools.partial(_mla_attn_kernel, n_tiles=n_t),
        name="mla_attention",
        grid=(b_sz, MLA_HEADS, s_len // MLA_TQ),
        in_specs=[
            pl.BlockSpec((None, MLA_QK, MLA_TQ), lambda b, h, i: (b, h, i)),
            pl.BlockSpec((None, t_len, MLA_QK), lambda b, h, i: (b, 0, h)),
            pl.BlockSpec((None, n_t, None, MLA_V, TM), lambda b, h, i: (b, 0, h, 0, 0)),
        ],
        out_specs=pl.BlockSpec((None, MLA_TQ, MLA_V), lambda b, h, i: (b, i, h)),
        out_shape=jax.ShapeDtypeStruct((b_sz, s_len, MLA_HEADS * MLA_V), BF16),
        scratch_shapes=[pltpu.VMEM((MLA_SLOTS, TM, MLA_TQ), F32), pltpu.VMEM((2, TM, MLA_TQ), BF16)],
        compiler_params=_params(("parallel", "parallel", "parallel")),
    )(q_t, k, v_t)


def _router_kernel(h_ref, wr_ref, br_ref, idx_ref, wt_ref):
    scores = jax.nn.sigmoid(_dot_nt(wr_ref[...], h_ref[...]))
    sel = scores + br_ref[...]
    sel_rows = [sel[e:e + 1, :] for e in range(N_EXPERTS)]
    score_rows = [scores[e:e + 1, :] for e in range(N_EXPERTS)]

    group_scores = []
    for g in range(N_GROUPS):
        a, b, c, d = sel_rows[g * GROUP_SZ:(g + 1) * GROUP_SZ]
        hi1, lo1 = jnp.maximum(a, b), jnp.minimum(a, b)
        hi2, lo2 = jnp.maximum(c, d), jnp.minimum(c, d)
        top1 = jnp.maximum(hi1, hi2)
        top2 = jnp.maximum(jnp.minimum(hi1, hi2), jnp.maximum(lo1, lo2))
        group_scores.append(top1 + top2)
    best = group_scores[0]
    gi = jnp.zeros_like(best, dtype=I32)
    for g in range(1, N_GROUPS):
        upd = group_scores[g] > best
        gi = jnp.where(upd, g, gi)
        best = jnp.where(upd, group_scores[g], best)

    def pick_group(rows_, j):
        out = rows_[j]
        for g in range(1, N_GROUPS):
            out = jnp.where(gi == g, rows_[g * GROUP_SZ + j], out)
        return out

    gsel = [pick_group(sel_rows, j) for j in range(GROUP_SZ)]
    gscore = [pick_group(score_rows, j) for j in range(GROUP_SZ)]

    def argmax_excluding(skip):
        val = jnp.full_like(best, -jnp.inf)
        idx = jnp.zeros_like(gi)
        wt = jnp.zeros_like(best)
        for j in range(GROUP_SZ):
            cand = gsel[j] if skip is None else jnp.where(skip == j, -jnp.inf, gsel[j])
            upd = cand > val
            idx = jnp.where(upd, j, idx)
            wt = jnp.where(upd, gscore[j], wt)
            val = jnp.where(upd, cand, val)
        return idx, wt

    i1, w1 = argmax_excluding(None)
    i2, w2 = argmax_excluding(i1)
    tot = w1 + w2
    idx_ref[0:1, :] = gi * GROUP_SZ + i1
    idx_ref[1:2, :] = gi * GROUP_SZ + i2
    wt_ref[0:1, :] = w1 / tot
    wt_ref[1:2, :] = w2 / tot


def _router(h16, wr_t, br):
    rows = h16.shape[0]
    return pl.pallas_call(
        _router_kernel,
        name="moe_router",
        grid=(rows // TM,),
        in_specs=[
            pl.BlockSpec((TM, D), lambda m: (m, 0)),
            pl.BlockSpec((N_EXPERTS, D), lambda m: (0, 0)),
            pl.BlockSpec((N_EXPERTS, 1), lambda m: (0, 0)),
        ],
        out_specs=[pl.BlockSpec((2, TM), lambda m: (0, m)), pl.BlockSpec((2, TM), lambda m: (0, m))],
        out_shape=[jax.ShapeDtypeStruct((2, rows), I32), jax.ShapeDtypeStruct((2, rows), F32)],
        compiler_params=_params(("parallel",)),
    )(h16, wr_t, br.reshape(N_EXPERTS, 1).astype(F32))


def _dispatch_plan(idx, rows):
    n_tiles = 2 * rows // TMG + N_EXPERTS
    e = idx.reshape(-1)
    onehot = (e[:, None] == jnp.arange(N_EXPERTS, dtype=I32)[None, :]).astype(I32)
    csum = jnp.cumsum(onehot, axis=0)
    rank = jnp.take_along_axis(csum, e[:, None], axis=1)[:, 0] - 1
    counts = csum[-1]
    padded = (counts + TMG - 1) // TMG * TMG
    ends = jnp.cumsum(padded)
    dest = (ends - padded)[e] + rank
    tok = jnp.tile(jnp.arange(rows, dtype=I32), 2)
    src = jnp.zeros((n_tiles * TMG,), I32).at[dest].set(tok)
    tile_start = jnp.arange(n_tiles, dtype=I32) * TMG
    tile_expert = jnp.minimum(
        jnp.sum((ends[None, :] <= tile_start[:, None]).astype(I32), axis=1), N_EXPERTS - 1)
    tile_valid = (tile_start < ends[-1]).astype(I32)
    return src, dest.reshape(2, rows).astype(I32), tile_expert, tile_valid, n_tiles


def _gmm_kernel(te_ref, tv_ref, src_ref, h_hbm, wg_ref, wu_ref, wd_ref, y_ref,
                buf0, buf1, wg_s, wu_s, wd_s, sem):
    i = pl.program_id(0)
    prev = jnp.maximum(i - 1, 0)
    bufs = (buf0, buf1)

    def gather(tile, slot):
        base = tile * TMG
        for r in range(TMG):
            pltpu.make_async_copy(
                h_hbm.at[pl.ds(src_ref[base + r], 1)], bufs[slot].at[pl.ds(r, 1)], sem.at[slot]
            ).start()

    def wait(slot):
        pltpu.make_async_copy(h_hbm.at[pl.ds(0, TMG)], bufs[slot], sem.at[slot]).wait()

    @pl.when(i == 0)
    def _():
        gather(0, 0)

    @pl.when((i == 0) | (te_ref[i] != te_ref[prev]))
    def _():
        wg_s[...] = wg_ref[...].astype(BF16)
        wu_s[...] = wu_ref[...].astype(BF16)
        wd_s[...] = wd_ref[...].astype(BF16)

    def compute(slot):
        wait(slot)
        x = bufs[slot][...].astype(BF16)
        gather(i + 1, 1 - slot)
        z = _silu(_dot(x, wg_s[...])) * _dot(x, wu_s[...])
        y_ref[...] = _dot(z.astype(BF16), wd_s[...])

    for parity in range(2):
        pl.when((tv_ref[i] > 0) & (i % 2 == parity))(functools.partial(compute, parity))

    @pl.when(tv_ref[i] == 0)
    def _():
        y_ref[...] = jnp.zeros_like(y_ref)

    for parity in range(2):
        pl.when((tv_ref[i] == 0) & (tv_ref[prev] > 0) & (i % 2 == parity))(
            functools.partial(wait, parity))


def _gmm(h32, src, tile_expert, tile_valid, n_tiles, layer, wg, wu, wd):
    expert = lambda i, te, tv, src_: (layer, te[i], 0, 0)
    grid_spec = pltpu.PrefetchScalarGridSpec(
        num_scalar_prefetch=3,
        grid=(n_tiles,),
        in_specs=[
            pl.BlockSpec(memory_space=pl.ANY),
            pl.BlockSpec((None, None, D, D_EXPERT), expert),
            pl.BlockSpec((None, None, D, D_EXPERT), expert),
            pl.BlockSpec((None, None, D_EXPERT, D), expert),
        ],
        out_specs=pl.BlockSpec((TMG, D), lambda i, te, tv, src_: (i, 0)),
        scratch_shapes=[
            pltpu.VMEM((TMG, D), F32),
            pltpu.VMEM((TMG, D), F32),
            pltpu.VMEM((D, D_EXPERT), BF16),
            pltpu.VMEM((D, D_EXPERT), BF16),
            pltpu.VMEM((D_EXPERT, D), BF16),
            pltpu.SemaphoreType.DMA((2,)),
        ],
    )
    return pl.pallas_call(
        _gmm_kernel,
        name="moe_experts",
        grid_spec=grid_spec,
        out_shape=jax.ShapeDtypeStruct((n_tiles * TMG, D), F32),
        compiler_params=_params(("arbitrary",)),
    )(tile_expert, tile_valid, src, h32, wg, wu, wd)


def _ffn_kernel(d0_ref, d1_ref, h_ref, wsg_ref, wsu_ref, wsd_ref, y_hbm, wt_ref, x_ref,
                gate_ref, gain_ref, *rest, n_t, with_next):
    if with_next:
        ngain_ref, nsc_ref, nsh_ref, o_ref, hn_ref, buf, sem = rest
    else:
        o_ref, buf, sem = rest
    base = (pl.program_id(0) * n_t + pl.program_id(1)) * TM
    for r in range(TM):
        pltpu.make_async_copy(y_hbm.at[pl.ds(d0_ref[base + r], 1)], buf.at[0, pl.ds(r, 1)], sem.at[0]).start()
        pltpu.make_async_copy(y_hbm.at[pl.ds(d1_ref[base + r], 1)], buf.at[1, pl.ds(r, 1)], sem.at[1]).start()
    h = h_ref[...]
    z = _silu(_dot(h, wsg_ref[...])) * _dot(h, wsu_ref[...])
    shared = _dot(z.astype(BF16), wsd_ref[...])
    for s in range(2):
        pltpu.make_async_copy(y_hbm.at[pl.ds(0, TM)], buf.at[s], sem.at[s]).wait()
    wt = wt_ref[...]
    fx = wt[:, 0:1] * buf[0] + wt[:, 1:2] * buf[1] + shared
    x = x_ref[...] + gate_ref[...] * (_rms(fx) * gain_ref[...])
    o_ref[...] = x
    if with_next:
        hn_ref[...] = _norm_mod(x, ngain_ref[...], nsc_ref[...], nsh_ref[...]).astype(BF16)


def _ffn_combine(h16, y, dest, wts, xs, wsg, wsu, wsd, gain, mods, n_lat, n_batch, nxt):
    b_sz, t_len, _ = xs.shape
    n_t = t_len // TM
    row = lambda b, t, d0, d1: (b, t, 0)
    const2 = lambda b, t, d0, d1: (0, 0)
    mod = lambda which: pl.BlockSpec(
        (None, None, 1, D),
        lambda b, t, d0, d1: (jnp.where(t >= n_lat, n_batch, b), which, 0, 0),
    )
    tile = pl.BlockSpec((None, TM, D), row)
    vec = pl.BlockSpec((1, D), const2)
    in_specs = [
        tile,
        pl.BlockSpec(wsg.shape, const2, pipeline_mode=pl.Buffered(1)),
        pl.BlockSpec(wsu.shape, const2, pipeline_mode=pl.Buffered(1)),
        pl.BlockSpec(wsd.shape, const2, pipeline_mode=pl.Buffered(1)),
        pl.BlockSpec(memory_space=pl.ANY),
        pl.BlockSpec((None, TM, 2), row),
        tile, mod(5), vec,
    ]
    args = [h16, wsg, wsu, wsd, y, wts.T.reshape(b_sz, t_len, 2), xs, mods, gain.reshape(1, D)]
    out_specs = [tile]
    out_shape = [jax.ShapeDtypeStruct((b_sz, t_len, D), F32)]
    if nxt is not None:
        in_specs += [vec, mod(1), mod(0)]
        args += [nxt[0].reshape(1, D), nxt[1], nxt[1]]
        out_specs.append(tile)
        out_shape.append(jax.ShapeDtypeStruct((b_sz, t_len, D), BF16))
    grid_spec = pltpu.PrefetchScalarGridSpec(
        num_scalar_prefetch=2,
        grid=(b_sz, n_t),
        in_specs=in_specs,
        out_specs=out_specs,
        scratch_shapes=[pltpu.VMEM((2, TM, D), F32), pltpu.SemaphoreType.DMA((2,))],
    )
    return pl.pallas_call(
        functools.partial(_ffn_kernel, n_t=n_t, with_next=nxt is not None),
        name="moe_shared_combine",
        grid_spec=grid_spec,
        out_shape=out_shape,
        compiler_params=_params(("arbitrary", "arbitrary")),
    )(dest[0], dest[1], *args)


def _moe_block(xs, h32, h16, gain, mods, n_lat, n_batch, wr_t, br, layer, wg, wu, wd,
               wsg, wsu, wsd, nxt):
    b_sz, t_len, _ = xs.shape
    rows = b_sz * t_len
    idx, wts = _router(h16.reshape(rows, D), wr_t, br)
    src, dest, tile_expert, tile_valid, n_tiles = _dispatch_plan(idx, rows)
    y = _gmm(h32.reshape(rows, D), src, tile_expert, tile_valid, n_tiles, layer, wg, wu, wd)
    return _ffn_combine(h16, y, dest, wts, xs, wsg, wsu, wsd, gain, mods, n_lat, n_batch, nxt)


def kernel(x, c, ctx, c_ctx, ada_w, ada_b, norm_g, w_in_e, w_out_e, na_rpb, wg_sink, w_in_o, mla_q_norm, mla_kv_norm, mla_w_qb, mla_w_kvb, w_out_o, w_router, b_router, moe_w_gate, moe_w_up, moe_w_down, shared_w_gate, shared_w_up, shared_w_down):
    b_sz, s_len, _ = x.shape
    c_len = ctx.shape[1]
    assert c_len == TM and s_len % MLA_TQ == 0 and b_sz < 8
    n_lat = s_len // TM
    bf = lambda a: a.astype(BF16)

    cc = jnp.zeros((8, D), F32).at[:b_sz].set(c).at[b_sz].set(c_ctx)
    mods = _ada_mods(cc, ada_w, ada_b)
    wr_t = bf(w_router.T)
    moe = lambda layer: (layer, moe_w_gate, moe_w_up, moe_w_down, bf(shared_w_gate[layer]),
                         bf(shared_w_up[layer]), bf(shared_w_down[layer]))

    xs, h = _norm_first(x, ctx, norm_g[0, 0], mods[0], b_sz)
    proj = _inproj_even(h, bf(w_in_e[0]), _rope_tables(s_len, c_len, HD))
    ya = _na_attention(proj, _na_bias_tables(na_rpb[0], s_len // GRID_W), s_len, c_len)
    yw = _wg_attention(proj, wg_sink[0], _wg_mask_tables(s_len), s_len, c_len)
    xs, h32, h16 = _outproj([ya, yw], bf(w_out_e[0]), xs, norm_g[0], mods[0], s_len + c_len, n_lat, b_sz)
    xs, h = _moe_block(xs, h32, h16, norm_g[0, 3], mods[0], n_lat, b_sz, wr_t, b_router, *moe(0),
                       (norm_g[1, 0], mods[1]))

    win = bf(jnp.pad(w_in_o[0], ((0, 0), (0, HD - MLA_ROPE))))
    wqb = bf(jnp.pad(mla_w_qb[0].reshape(MLA_RANK, MLA_HEADS, MLA_NOPE + MLA_ROPE),
                     ((0, 0), (0, 0), (0, MLA_QK - MLA_NOPE - MLA_ROPE))).reshape(MLA_RANK, -1))
    wkvb = mla_w_kvb[0].reshape(MLA_RANK, MLA_HEADS, MLA_NOPE + MLA_V)
    wkn = bf(wkvb[:, :, :MLA_NOPE].reshape(MLA_RANK, -1))
    wv = bf(wkvb[:, :, MLA_NOPE:].reshape(MLA_RANK, -1))
    q_t, k, v_t = _mla_in(h, win, wqb.T, wkn, wv.T, mla_q_norm[0], mla_kv_norm[0],
                          _rope_tables(s_len, c_len, MLA_ROPE))
    y = _mla_attention(q_t, k, v_t, s_len, c_len)
    xl, h32, h16 = _outproj([y], bf(w_out_o[0]), xs, norm_g[1], mods[1], s_len, n_lat, b_sz)
    (out,) = _moe_block(xl, h32, h16, norm_g[1, 3], mods[1], n_lat, b_sz, wr_t, b_router, *moe(1), None)
    return out
```

```python
import functools

import numpy as np
import jax
import jax.numpy as jnp
from jax import lax
from jax.experimental import pallas as pl
from jax.experimental.pallas import tpu as pltpu

F32 = jnp.float32
BF16 = jnp.bfloat16
I32 = jnp.int32

D = 2048
GRID_W = 64
HD = 128
EPS = 1e-6
NEG = -1e30
ROPE_THETA = 10000.0
LOG2E = 1.4426950408889634

NA_HEADS = 8
NA_WIN_ROWS = 8
NA_WIN_COLS = 16
NA_QROWS = 4
NA_KROWS = NA_QROWS + NA_WIN_ROWS
WG_HEADS = 8
WG_KV = 2
WG_GRP = WG_HEADS // WG_KV
WG_WINDOW = 128
WG_TQ = 256
WG_SPAN = WG_TQ + 2 * WG_WINDOW
EVEN_IN = (3 * NA_HEADS + WG_HEADS + 2 * WG_KV) * HD
MLA_HEADS = 16
MLA_RANK = 512
MLA_NOPE = 128
MLA_ROPE = 64
MLA_V = 128
MLA_QK = 256
MLA_TQ = 1024
N_EXPERTS = 16
N_GROUPS = 4
GROUP_SZ = N_EXPERTS // N_GROUPS
D_EXPERT = 512
D_SHARED = 1024

TM = 256
TMG = 256
VMEM_LIMIT = 56 * 1024 * 1024


def _params(sem, vmem=VMEM_LIMIT):
    return pltpu.CompilerParams(dimension_semantics=sem, vmem_limit_bytes=vmem)


def _dot(a, b):
    return jnp.dot(a, b, preferred_element_type=F32)


def _dot_nt(a, b):
    return lax.dot_general(a, b, (((1,), (1,)), ((), ())), preferred_element_type=F32)


def _resident(shape):
    nd = len(shape)
    return pl.BlockSpec(shape, lambda *_: (0,) * nd, pipeline_mode=pl.Buffered(1))


def _silu(a):
    return a * jax.nn.sigmoid(a)


def _rms(x):
    return x * lax.rsqrt(jnp.mean(x * x, axis=-1, keepdims=True) + EPS)


def _ada_kernel(c_ref, w_ref, b_ref, o_ref):
    s = _silu(c_ref[...]).astype(BF16)
    o_ref[...] = _dot(s, w_ref[...].astype(BF16)) + b_ref[...]


def _ada_mods(cc, ada_w, ada_b):
    n_layers, _, n_out = ada_w.shape
    tn = 1024
    out = pl.pallas_call(
        _ada_kernel,
        name="ada_mods",
        grid=(n_layers, n_out // tn),
        in_specs=[
            pl.BlockSpec((8, D), lambda l, j: (0, 0)),
            pl.BlockSpec((None, D, tn), lambda l, j: (l, 0, j)),
            pl.BlockSpec((None, 1, tn), lambda l, j: (l, 0, j)),
        ],
        out_specs=pl.BlockSpec((None, 8, tn), lambda l, j: (l, 0, j)),
        out_shape=jax.ShapeDtypeStruct((n_layers, 8, n_out), F32),
        compiler_params=_params(("parallel", "parallel")),
    )(cc, ada_w, ada_b.reshape(n_layers, 1, n_out))
    return out.reshape(n_layers, 8, 6, 1, D)


def _mod_spec(which, n_lat_tiles, ctx_row):
    return pl.BlockSpec(
        (None, None, 1, D),
        lambda b, t: (jnp.where(t >= n_lat_tiles, ctx_row, b), which, 0, 0),
    )


def _norm_mod(x, g, sc, sh):
    return (_rms(x) * g) * (1.0 + sc) + sh


def _norm_first_kernel(x_ref, c_ref, g_ref, sc_ref, sh_ref, xs_ref, h_ref, *, n_lat):
    t = pl.program_id(1)

    def emit(src):
        x = src[...]
        xs_ref[...] = x
        h_ref[...] = _norm_mod(x, g_ref[...], sc_ref[...], sh_ref[...]).astype(BF16)

    @pl.when(t < n_lat)
    def _():
        emit(x_ref)

    @pl.when(t == n_lat)
    def _():
        emit(c_ref)


def _norm_first(x, ctx, gain, mods, n_batch):
    b_sz, s_len, _ = x.shape
    c_len = ctx.shape[1]
    n_lat = s_len // TM
    t_len = s_len + c_len
    row = lambda b, t: (b, t, 0)
    return pl.pallas_call(
        functools.partial(_norm_first_kernel, n_lat=n_lat),
        name="norm_first",
        grid=(b_sz, n_lat + 1),
        in_specs=[
            pl.BlockSpec((None, TM, D), lambda b, t: (b, jnp.minimum(t, n_lat - 1), 0)),
            pl.BlockSpec((None, TM, D), lambda b, t: (b, 0, 0)),
            pl.BlockSpec((1, D), lambda b, t: (0, 0)),
            _mod_spec(1, n_lat, n_batch),
            _mod_spec(0, n_lat, n_batch),
        ],
        out_specs=[pl.BlockSpec((None, TM, D), row), pl.BlockSpec((None, TM, D), row)],
        out_shape=[
            jax.ShapeDtypeStruct((b_sz, t_len, D), F32),
            jax.ShapeDtypeStruct((b_sz, t_len, D), BF16),
        ],
        compiler_params=_params(("parallel", "arbitrary")),
    )(x, ctx, gain.reshape(1, D), mods, mods)


def _rope_tables(s_len, c_len, rot_dim):
    quarter = rot_dim // 4
    t = jnp.arange(s_len, dtype=I32)
    row = (t // GRID_W).astype(F32)
    col = (t % GRID_W).astype(F32)
    inv_freq = ROPE_THETA ** (-jnp.arange(quarter, dtype=F32) / quarter)
    ang_r = row[:, None] * inv_freq[None, :]
    ang_c = col[:, None] * inv_freq[None, :]
    ang = jnp.concatenate([ang_r, ang_r, ang_c, ang_c], axis=-1)
    cos, sin = jnp.cos(ang), jnp.sin(ang)
    even = ((jnp.arange(rot_dim) // quarter) % 2 == 0)[None, :]
    sa = jnp.where(even, -sin, 0.0)
    sb = jnp.where(even, 0.0, sin)
    pad = ((0, c_len), (0, HD - rot_dim))
    return (
        jnp.pad(cos, pad, constant_values=1.0),
        jnp.pad(sa, pad),
        jnp.pad(sb, pad),
    )


def _rope(x, cos, sa, sb, quarter):
    return x * cos + pltpu.roll(x, HD - quarter, 1) * sa + pltpu.roll(x, quarter, 1) * sb


EVEN_TN = 1536
EVEN_ROPE_BLOCK = 2
EVEN_ROPE_CHUNKS = WG_HEADS + WG_KV


ATTN_QSCALE = HD ** -0.5 * LOG2E


def _inproj_even_kernel(h_ref, w_ref, cos_ref, sa_ref, sb_ref, o_ref):
    j = pl.program_id(0)
    acc = _dot(h_ref[...], w_ref[...])
    n_q = NA_HEADS * HD

    @pl.when(j == 0)
    def _():
        o_ref[:, :n_q] = (acc[:, :n_q] * ATTN_QSCALE).astype(BF16)
        o_ref[:, n_q:] = acc[:, n_q:].astype(BF16)

    @pl.when((j != 0) & (j != EVEN_ROPE_BLOCK))
    def _():
        o_ref[...] = acc.astype(BF16)

    @pl.when(j == EVEN_ROPE_BLOCK)
    def _():
        cos, sa, sb = cos_ref[...], sa_ref[...], sb_ref[...]
        for ch in range(EVEN_ROPE_CHUNKS):
            sl = slice(ch * HD, (ch + 1) * HD)
            r = _rope(acc[:, sl], cos, sa, sb, HD // 4)
            o_ref[:, sl] = (r * ATTN_QSCALE if ch < WG_HEADS else r).astype(BF16)
        rest = slice(EVEN_ROPE_CHUNKS * HD, EVEN_TN)
        o_ref[:, rest] = acc[:, rest].astype(BF16)


def _inproj_even(h, w, tables):
    b_sz, t_len, _ = h.shape
    n_t = t_len // TM
    rows = b_sz * t_len
    tab = pl.BlockSpec((TM, HD), lambda j, m: (m % n_t, 0))
    out = pl.pallas_call(
        _inproj_even_kernel,
        name="inproj_even",
        grid=(EVEN_IN // EVEN_TN, rows // TM),
        in_specs=[
            pl.BlockSpec((TM, D), lambda j, m: (m, 0)),
            pl.BlockSpec((D, EVEN_TN), lambda j, m: (0, j)),
            tab, tab, tab,
        ],
        out_specs=pl.BlockSpec((TM, EVEN_TN), lambda j, m: (m, j)),
        out_shape=jax.ShapeDtypeStruct((rows, EVEN_IN), BF16),
        compiler_params=_params(("parallel", "parallel")),
    )(h.reshape(rows, D), w, *tables)
    return out.reshape(b_sz, t_len, EVEN_IN)


def _na_bias_tables(rpb, rows):
    n_heads, n_dr, n_dc = rpb.shape
    w = GRID_W
    centre = NA_WIN_COLS - 1
    vec = jnp.pad(rpb.astype(F32) * LOG2E, ((0, 0), (0, 0), (w - 1 - centre, w - n_dc + centre + 1)))
    toe = jnp.tile(vec, (1, 1, w))[:, :, :w * (2 * w - 1)].reshape(n_heads, n_dr, w, 2 * w - 1)
    toe = toe[:, :, :, w - 1:]
    qc = np.arange(w)
    c_start = np.clip(qc - NA_WIN_COLS // 2, 0, w - NA_WIN_COLS)
    col_ok = (qc[None, :] >= c_start[:, None]) & (qc[None, :] < c_start[:, None] + NA_WIN_COLS)
    n_tiles = rows // NA_QROWS
    tabs = []
    for i in (0, 1, n_tiles - 1):
        r0 = min(max(NA_QROWS * i - NA_WIN_ROWS // 2, 0), rows - NA_KROWS)
        blocks = []
        for qr in range(NA_QROWS * i, NA_QROWS * (i + 1)):
            r_start = min(max(qr - NA_WIN_ROWS // 2, 0), rows - NA_WIN_ROWS)
            per_kr = []
            for kr in range(r0, r0 + NA_KROWS):
                if r_start <= kr < r_start + NA_WIN_ROWS:
                    per_kr.append(jnp.where(col_ok[None], toe[:, kr - qr + NA_WIN_ROWS - 1], NEG))
                else:
                    per_kr.append(jnp.full((n_heads, w, w), NEG, F32))
            blocks.append(jnp.concatenate(per_kr, axis=-1))
        tabs.append(jnp.concatenate(blocks, axis=1))
    return jnp.stack(tabs, axis=1)


def _softmax_pv(parts, extra_logit=None):
    m = parts[0][0].max(axis=-1, keepdims=True)
    for s, _ in parts[1:]:
        m = jnp.maximum(m, s.max(axis=-1, keepdims=True))
    if extra_logit is not None:
        m = jnp.maximum(m, extra_logit)
    l = None
    o = None
    for s, v in parts:
        p = jnp.exp2(s - m)
        ls = p.sum(axis=-1, keepdims=True)
        os_ = _dot(p.astype(BF16), v)
        l = ls if l is None else l + ls
        o = os_ if o is None else o + os_
    if extra_logit is not None:
        l = l + jnp.exp2(extra_logit - m)
    return o / l


def _na_kernel(q_ref, k_ref, v_ref, bias_ref, o_ref, *, s_len, c_len):
    rows = s_len // GRID_W
    tq = NA_QROWS * GRID_W
    tk = NA_KROWS * GRID_W
    n_q = s_len // tq
    kc = k_ref[s_len:s_len + c_len, :]
    vc = v_ref[s_len:s_len + c_len, :]

    def body(i, carry):
        qs = pl.multiple_of(i * tq, tq)
        r0 = jnp.clip(NA_QROWS * i - NA_WIN_ROWS // 2, 0, rows - NA_KROWS)
        ks = pl.multiple_of(r0 * GRID_W, tq)
        typ = jnp.where(i == 0, 0, jnp.where(i == n_q - 1, 2, 1))
        q = q_ref[pl.ds(qs, tq), :]
        s_loc = _dot_nt(q, k_ref[pl.ds(ks, tk), :]) + bias_ref[typ]
        s_ctx = _dot_nt(q, kc)
        o = _softmax_pv([(s_loc, v_ref[pl.ds(ks, tk), :]), (s_ctx, vc)])
        o_ref[pl.ds(qs, tq), :] = o.astype(BF16)
        return carry

    lax.fori_loop(0, n_q, body, 0)
    s_cc = _dot_nt(q_ref[s_len:s_len + c_len, :], kc)
    o_ref[s_len:s_len + c_len, :] = _softmax_pv([(s_cc, vc)]).astype(BF16)


def _na_attention(proj, bias, s_len, c_len):
    b_sz, t_len, _ = proj.shape
    col = lambda off: pl.BlockSpec((None, t_len, HD), lambda b, h: (b, 0, off + h))
    return pl.pallas_call(
        functools.partial(_na_kernel, s_len=s_len, c_len=c_len),
        name="na_attention",
        grid=(b_sz, NA_HEADS),
        in_specs=[
            col(0), col(NA_HEADS), col(2 * NA_HEADS),
            pl.BlockSpec((None,) + bias.shape[1:], lambda b, h: (h, 0, 0, 0)),
        ],
        out_specs=pl.BlockSpec((None, t_len, HD), lambda b, h: (b, 0, h)),
        out_shape=jax.ShapeDtypeStruct((b_sz, t_len, NA_HEADS * HD), BF16),
        compiler_params=_params(("parallel", "parallel")),
    )(proj, proj, proj, bias)


def _wg_mask_tables(s_len):
    n_q = s_len // WG_TQ
    tabs = []
    for i in (0, 1, n_q - 1):
        k0 = min(max(WG_TQ * i - WG_WINDOW, 0), s_len - WG_SPAN)
        rel = (k0 + jnp.arange(WG_SPAN))[None, :] - (WG_TQ * i + jnp.arange(WG_TQ))[:, None]
        tabs.append(jnp.where(jnp.abs(rel) <= WG_WINDOW, 0.0, NEG).astype(F32))
    return jnp.stack(tabs)


def _wg_kernel(sink_ref, q_ref, k_ref, v_ref, mask_ref, o_ref, *, s_len, c_len):
    kvh = pl.program_id(1)
    n_q = s_len // WG_TQ
    kc = k_ref[s_len:s_len + c_len, :]
    vc = v_ref[s_len:s_len + c_len, :]
    sinks = [sink_ref[kvh * WG_GRP + g] * LOG2E for g in range(WG_GRP)]

    def body(i, carry):
        qs = pl.multiple_of(i * WG_TQ, WG_TQ)
        ks = pl.multiple_of(jnp.clip(WG_TQ * i - WG_WINDOW, 0, s_len - WG_SPAN), WG_WINDOW)
        typ = jnp.where(i == 0, 0, jnp.where(i == n_q - 1, 2, 1))
        kl = k_ref[pl.ds(ks, WG_SPAN), :]
        vl = v_ref[pl.ds(ks, WG_SPAN), :]
        mask = mask_ref[typ]
        for g in range(WG_GRP):
            sl = slice(g * HD, (g + 1) * HD)
            q = q_ref[pl.ds(qs, WG_TQ), sl]
            o = _softmax_pv([(_dot_nt(q, kl) + mask, vl), (_dot_nt(q, kc), vc)], sinks[g])
            o_ref[pl.ds(qs, WG_TQ), sl] = o.astype(BF16)
        return carry

    lax.fori_loop(0, n_q, body, 0)
    for g in range(WG_GRP):
        sl = slice(g * HD, (g + 1) * HD)
        o = _softmax_pv([(_dot_nt(q_ref[s_len:s_len + c_len, sl], kc), vc)], sinks[g])
        o_ref[s_len:s_len + c_len, sl] = o.astype(BF16)


def _wg_attention(proj, sink, mask, s_len, c_len):
    b_sz, t_len, _ = proj.shape
    q_off = 3 * NA_HEADS * HD // (WG_GRP * HD)
    k_off = 3 * NA_HEADS + WG_HEADS
    v_off = k_off + WG_KV
    return pl.pallas_call(
        functools.partial(_wg_kernel, s_len=s_len, c_len=c_len),
        name="wg_attention",
        grid=(b_sz, WG_KV),
        in_specs=[
            pl.BlockSpec(memory_space=pltpu.SMEM),
            pl.BlockSpec((None, t_len, WG_GRP * HD), lambda b, h: (b, 0, q_off + h)),
            pl.BlockSpec((None, t_len, HD), lambda b, h: (b, 0, k_off + h)),
            pl.BlockSpec((None, t_len, HD), lambda b, h: (b, 0, v_off + h)),
            pl.BlockSpec(mask.shape, lambda b, h: (0, 0, 0)),
        ],
        out_specs=pl.BlockSpec((None, t_len, WG_GRP * HD), lambda b, h: (b, 0, h)),
        out_shape=jax.ShapeDtypeStruct((b_sz, t_len, WG_HEADS * HD), BF16),
        compiler_params=_params(("parallel", "parallel")),
    )(sink.astype(F32), proj, proj, proj, mask)


def _outproj_kernel(*refs, n_parts):
    y_refs = refs[:n_parts]
    (w_ref, x_ref, gate_ref, gain_ref, ngain_ref, nsc_ref, nsh_ref,
     o_ref, h32_ref, h16_ref) = refs[n_parts:]
    y = None
    off = 0
    for y_ref in y_refs:
        k = y_ref.shape[-1]
        part = _dot(y_ref[...], w_ref[off:off + k, :])
        y = part if y is None else y + part
        off += k
    x = x_ref[...] + gate_ref[...] * (_rms(y) * gain_ref[...])
    o_ref[...] = x
    h = _norm_mod(x, ngain_ref[...], nsc_ref[...], nsh_ref[...])
    h32_ref[...] = h
    h16_ref[...] = h.astype(BF16)


def _outproj(ys, w, xs, gains, mods, n_rows, n_lat, n_batch):
    b_sz = xs.shape[0]
    row = lambda b, t: (b, t, 0)
    vec = pl.BlockSpec((1, D), lambda b, t: (0, 0))
    tile = pl.BlockSpec((None, TM, D), row)
    return pl.pallas_call(
        functools.partial(_outproj_kernel, n_parts=len(ys)),
        name="outproj_residual",
        grid=(b_sz, n_rows // TM),
        in_specs=[pl.BlockSpec((None, TM, y.shape[-1]), row) for y in ys] + [
            _resident(w.shape), tile, _mod_spec(2, n_lat, n_batch), vec,
            vec, _mod_spec(4, n_lat, n_batch), _mod_spec(3, n_lat, n_batch),
        ],
        out_specs=[tile, tile, tile],
        out_shape=[
            jax.ShapeDtypeStruct((b_sz, n_rows, D), F32),
            jax.ShapeDtypeStruct((b_sz, n_rows, D), F32),
            jax.ShapeDtypeStruct((b_sz, n_rows, D), BF16),
        ],
        compiler_params=_params(("parallel", "parallel")),
    )(*ys, w, xs, mods, gains[1].reshape(1, D), gains[2].reshape(1, D), mods, mods)


def _roll_rows(x, shift):
    return jnp.concatenate([x[-shift:], x[:-shift]], axis=0)


def _mla_in_kernel(h_ref, win_ref, wqbt_ref, wkn_ref, wvt_ref, qg_ref, kvg_ref,
                   cos_ref, sa_ref, sb_ref, cost_ref, sat_ref, sbt_ref, q_ref, k_ref, v_ref):
    quarter = MLA_ROPE // 4
    c = _dot(h_ref[...], win_ref[...])
    cq = c[:, :MLA_RANK]
    ckv = c[:, MLA_RANK:2 * MLA_RANK]
    k_rope = _rope(c[:, 2 * MLA_RANK:], cos_ref[...], sa_ref[...], sb_ref[...], quarter).astype(BF16)
    q_scale = (MLA_NOPE + MLA_ROPE) ** -0.5 * LOG2E
    q_t = _dot_nt(wqbt_ref[...], (_rms(cq) * qg_ref[...]).astype(BF16))
    kvn = (_rms(ckv) * kvg_ref[...]).astype(BF16)
    k_nope = _dot(kvn, wkn_ref[...])
    v_t = _dot_nt(wvt_ref[...], kvn)
    cos_t, sa_t, sb_t = cost_ref[...], sat_ref[...], sbt_ref[...]
    for h in range(MLA_HEADS):
        lo = slice(h * MLA_QK, h * MLA_QK + HD)
        hi = slice(h * MLA_QK + HD, (h + 1) * MLA_QK)
        x = q_t[hi]
        x = x * cos_t + _roll_rows(x, HD - quarter) * sa_t + _roll_rows(x, quarter) * sb_t
        q_ref[lo, :] = (q_t[lo] * q_scale).astype(BF16)
        q_ref[hi, :] = (x * q_scale).astype(BF16)
        k_ref[:, lo] = k_nope[:, h * HD:(h + 1) * HD].astype(BF16)
        k_ref[:, hi] = k_rope
        v_ref[h] = v_t[h * MLA_V:(h + 1) * MLA_V].astype(BF16)


def _mla_in(h, win, wqbt, wkn, wvt, q_gain, kv_gain, tables):
    b_sz, t_len, _ = h.shape
    n_t = t_len // TM
    tab = pl.BlockSpec((TM, HD), lambda b, t: (t, 0))
    tab_t = pl.BlockSpec((HD, TM), lambda b, t: (0, t))
    tables_t = tuple(a.T for a in tables)
    return pl.pallas_call(
        _mla_in_kernel,
        name="mla_in",
        grid=(b_sz, n_t),
        in_specs=[
            pl.BlockSpec((None, TM, D), lambda b, t: (b, t, 0)),
            _resident(win.shape), _resident(wqbt.shape), _resident(wkn.shape),
            _resident(wvt.shape), _resident((1, MLA_RANK)), _resident((1, MLA_RANK)),
            tab, tab, tab, tab_t, tab_t, tab_t,
        ],
        out_specs=[
            pl.BlockSpec((None, MLA_HEADS * MLA_QK, TM), lambda b, t: (b, 0, t)),
            pl.BlockSpec((None, TM, MLA_HEADS * MLA_QK), lambda b, t: (b, t, 0)),
            pl.BlockSpec((None, None, MLA_HEADS, MLA_V, TM), lambda b, t: (b, t, 0, 0, 0)),
        ],
        out_shape=[
            jax.ShapeDtypeStruct((b_sz, MLA_HEADS * MLA_QK, t_len), BF16),
            jax.ShapeDtypeStruct((b_sz, t_len, MLA_HEADS * MLA_QK), BF16),
            jax.ShapeDtypeStruct((b_sz, n_t, MLA_HEADS, MLA_V, TM), BF16),
        ],
        compiler_params=_params(("parallel", "parallel")),
    )(h, win, wqbt, wkn, wvt, q_gain.reshape(1, MLA_RANK), kv_gain.reshape(1, MLA_RANK),
      *tables, *tables_t)


MLA_SUB = 32
MLA_SLOTS = 3


def _mla_attn_kernel(q_ref, k_ref, v_ref, o_ref, s_scr, p_scr, *, n_tiles):
    q_t = q_ref[...]

    def logits(tile, slot):
        ks = pl.multiple_of(tile * TM, TM)
        s = _dot(k_ref[pl.ds(ks, TM), :], q_t)
        s_scr[slot] = s
        return s.max(axis=0, keepdims=True)

    def softmax(s_slot, p_slot, m_tile, m, l):
        m_new = jnp.maximum(m, m_tile)
        alpha = jnp.exp2(m - m_new)
        lsum = jnp.zeros((8, MLA_TQ), F32)
        for r in range(0, TM, MLA_SUB):
            p = jnp.exp2(s_scr[s_slot, r:r + MLA_SUB, :] - m_new)
            for g in range(0, MLA_SUB, 8):
                lsum = lsum + p[g:g + 8]
            p_scr[p_slot, r:r + MLA_SUB, :] = p.astype(BF16)
        return m_new, alpha, alpha * l + lsum.sum(axis=0, keepdims=True)

    m = jnp.full((1, MLA_TQ), -jnp.inf, F32)
    l = jnp.zeros((1, MLA_TQ), F32)
    acc = jnp.zeros((MLA_V, MLA_TQ), F32)
    alpha_prev = None
    m_tile = logits(0, 0)
    for t in range(n_tiles):
        m_next = logits(t + 1, (t + 1) % MLA_SLOTS) if t + 1 < n_tiles else None
        if t > 0:
            pv_prev = _dot(v_ref[t - 1], p_scr[(t - 1) % 2])
        m, alpha, l = softmax(t % MLA_SLOTS, t % 2, m_tile, m, l)
        if t > 0:
            acc = alpha_prev * acc + pv_prev
        alpha_prev, m_tile = alpha, m_next
    acc = alpha_prev * acc + _dot(v_ref[n_tiles - 1], p_scr[(n_tiles - 1) % 2])
    o_ref[...] = (acc / l).T.astype(BF16)


def _mla_attention(q_t, k, v_t, s_len, c_len):
    b_sz, t_len, _ = k.shape
    n_t = t_len // TM
    return pl.pallas_call(
        functools.partial(_mla_attn_kernel, n_tiles=n_t),
        name="mla_attention",
        grid=(b_sz, MLA_HEADS, s_len // MLA_TQ),
        in_specs=[
            pl.BlockSpec((None, MLA_QK, MLA_TQ), lambda b, h, i: (b, h, i)),
            pl.BlockSpec((None, t_len, MLA_QK), lambda b, h, i: (b, 0, h)),
            pl.BlockSpec((None, n_t, None, MLA_V, TM), lambda b, h, i: (b, 0, h, 0, 0)),
        ],
        out_specs=pl.BlockSpec((None, MLA_TQ, MLA_V), lambda b, h, i: (b, i, h)),
        out_shape=jax.ShapeDtypeStruct((b_sz, s_len, MLA_HEADS * MLA_V), BF16),
        scratch_shapes=[pltpu.VMEM((MLA_SLOTS, TM, MLA_TQ), F32), pltpu.VMEM((2, TM, MLA_TQ), BF16)],
        compiler_params=_params(("parallel", "parallel", "parallel")),
    )(q_t, k, v_t)


def _router_kernel(h_ref, wr_ref, br_ref, idx_ref, wt_ref):
    scores = jax.nn.sigmoid(_dot_nt(wr_ref[...], h_ref[...]))
    sel = scores + br_ref[...]
    sel_rows = [sel[e:e + 1, :] for e in range(N_EXPERTS)]
    score_rows = [scores[e:e + 1, :] for e in range(N_EXPERTS)]

    group_scores = []
    for g in range(N_GROUPS):
        a, b, c, d = sel_rows[g * GROUP_SZ:(g + 1) * GROUP_SZ]
        hi1, lo1 = jnp.maximum(a, b), jnp.minimum(a, b)
        hi2, lo2 = jnp.maximum(c, d), jnp.minimum(c, d)
        top1 = jnp.maximum(hi1, hi2)
        top2 = jnp.maximum(jnp.minimum(hi1, hi2), jnp.maximum(lo1, lo2))
        group_scores.append(top1 + top2)
    best = group_scores[0]
    gi = jnp.zeros_like(best, dtype=I32)
    for g in range(1, N_GROUPS):
        upd = group_scores[g] > best
        gi = jnp.where(upd, g, gi)
        best = jnp.where(upd, group_scores[g], best)

    def pick_group(rows_, j):
        out = rows_[j]
        for g in range(1, N_GROUPS):
            out = jnp.where(gi == g, rows_[g * GROUP_SZ + j], out)
        return out

    gsel = [pick_group(sel_rows, j) for j in range(GROUP_SZ)]
    gscore = [pick_group(score_rows, j) for j in range(GROUP_SZ)]

    def argmax_excluding(skip):
        val = jnp.full_like(best, -jnp.inf)
        idx = jnp.zeros_like(gi)
        wt = jnp.zeros_like(best)
        for j in range(GROUP_SZ):
            cand = gsel[j] if skip is None else jnp.where(skip == j, -jnp.inf, gsel[j])
            upd = cand > val
            idx = jnp.where(upd, j, idx)
            wt = jnp.where(upd, gscore[j], wt)
            val = jnp.where(upd, cand, val)
        return idx, wt

    i1, w1 = argmax_excluding(None)
    i2, w2 = argmax_excluding(i1)
    tot = w1 + w2
    idx_ref[0:1, :] = gi * GROUP_SZ + i1
    idx_ref[1:2, :] = gi * GROUP_SZ + i2
    wt_ref[0:1, :] = w1 / tot
    wt_ref[1:2, :] = w2 / tot


def _router(h16, wr_t, br):
    rows = h16.shape[0]
    return pl.pallas_call(
        _router_kernel,
        name="moe_router",
        grid=(rows // TM,),
        in_specs=[
            pl.BlockSpec((TM, D), lambda m: (m, 0)),
            pl.BlockSpec((N_EXPERTS, D), lambda m: (0, 0)),
            pl.BlockSpec((N_EXPERTS, 1), lambda m: (0, 0)),
        ],
        out_specs=[pl.BlockSpec((2, TM), lambda m: (0, m)), pl.BlockSpec((2, TM), lambda m: (0, m))],
        out_shape=[jax.ShapeDtypeStruct((2, rows), I32), jax.ShapeDtypeStruct((2, rows), F32)],
        compiler_params=_params(("parallel",)),
    )(h16, wr_t, br.reshape(N_EXPERTS, 1).astype(F32))


def _dispatch_plan(idx, rows):
    n_tiles = 2 * rows // TMG + N_EXPERTS
    e = idx.reshape(-1)
    onehot = (e[:, None] == jnp.arange(N_EXPERTS, dtype=I32)[None, :]).astype(I32)
    csum = jnp.cumsum(onehot, axis=0)
    rank = jnp.take_along_axis(csum, e[:, None], axis=1)[:, 0] - 1
    counts = csum[-1]
    padded = (counts + TMG - 1) // TMG * TMG
    ends = jnp.cumsum(padded)
    dest = (ends - padded)[e] + rank
    tok = jnp.tile(jnp.arange(rows, dtype=I32), 2)
    src = jnp.zeros((n_tiles * TMG,), I32).at[dest].set(tok)
    tile_start = jnp.arange(n_tiles, dtype=I32) * TMG
    tile_expert = jnp.minimum(
        jnp.sum((ends[None, :] <= tile_start[:, None]).astype(I32), axis=1), N_EXPERTS - 1)
    tile_valid = (tile_start < ends[-1]).astype(I32)
    return src, dest.reshape(2, rows).astype(I32), tile_expert, tile_valid, n_tiles


def _gmm_kernel(te_ref, tv_ref, src_ref, h_hbm, wg_ref, wu_ref, wd_ref, y_ref,
                buf0, buf1, wg_s, wu_s, wd_s, sem):
    i = pl.program_id(0)
    prev = jnp.maximum(i - 1, 0)
    bufs = (buf0, buf1)

    def gather(tile, slot):
        base = tile * TMG
        for r in range(TMG):
            pltpu.make_async_copy(
                h_hbm.at[pl.ds(src_ref[base + r], 1)], bufs[slot].at[pl.ds(r, 1)], sem.at[slot]
            ).start()

    def wait(slot):
        pltpu.make_async_copy(h_hbm.at[pl.ds(0, TMG)], bufs[slot], sem.at[slot]).wait()

    @pl.when(i == 0)
    def _():
        gather(0, 0)

    @pl.when((i == 0) | (te_ref[i] != te_ref[prev]))
    def _():
        wg_s[...] = wg_ref[...].astype(BF16)
        wu_s[...] = wu_ref[...].astype(BF16)
        wd_s[...] = wd_ref[...].astype(BF16)

    def compute(slot):
        wait(slot)
        x = bufs[slot][...].astype(BF16)
        gather(i + 1, 1 - slot)
        z = _silu(_dot(x, wg_s[...])) * _dot(x, wu_s[...])
        y_ref[...] = _dot(z.astype(BF16), wd_s[...])

    for parity in range(2):
        pl.when((tv_ref[i] > 0) & (i % 2 == parity))(functools.partial(compute, parity))

    @pl.when(tv_ref[i] == 0)
    def _():
        y_ref[...] = jnp.zeros_like(y_ref)

    for parity in range(2):
        pl.when((tv_ref[i] == 0) & (tv_ref[prev] > 0) & (i % 2 == parity))(
            functools.partial(wait, parity))


def _gmm(h32, src, tile_expert, tile_valid, n_tiles, layer, wg, wu, wd):
    expert = lambda i, te, tv, src_: (layer, te[i], 0, 0)
    grid_spec = pltpu.PrefetchScalarGridSpec(
        num_scalar_prefetch=3,
        grid=(n_tiles,),
        in_specs=[
            pl.BlockSpec(memory_space=pl.ANY),
            pl.BlockSpec((None, None, D, D_EXPERT), expert),
            pl.BlockSpec((None, None, D, D_EXPERT), expert),
            pl.BlockSpec((None, None, D_EXPERT, D), expert),
        ],
        out_specs=pl.BlockSpec((TMG, D), lambda i, te, tv, src_: (i, 0)),
        scratch_shapes=[
            pltpu.VMEM((TMG, D), F32),
            pltpu.VMEM((TMG, D), F32),
            pltpu.VMEM((D, D_EXPERT), BF16),
            pltpu.VMEM((D, D_EXPERT), BF16),
            pltpu.VMEM((D_EXPERT, D), BF16),
            pltpu.SemaphoreType.DMA((2,)),
        ],
    )
    return pl.pallas_call(
        _gmm_kernel,
        name="moe_experts",
        grid_spec=grid_spec,
        out_shape=jax.ShapeDtypeStruct((n_tiles * TMG, D), F32),
        compiler_params=_params(("arbitrary",)),
    )(tile_expert, tile_valid, src, h32, wg, wu, wd)


def _ffn_kernel(d0_ref, d1_ref, h_ref, wsg_ref, wsu_ref, wsd_ref, y_hbm, wt_ref, x_ref,
                gate_ref, gain_ref, *rest, n_t, with_next):
    if with_next:
        ngain_ref, nsc_ref, nsh_ref, o_ref, hn_ref, buf, sem = rest
    else:
        o_ref, buf, sem = rest
    base = (pl.program_id(0) * n_t + pl.program_id(1)) * TM
    for r in range(TM):
        pltpu.make_async_copy(y_hbm.at[pl.ds(d0_ref[base + r], 1)], buf.at[0, pl.ds(r, 1)], sem.at[0]).start()
        pltpu.make_async_copy(y_hbm.at[pl.ds(d1_ref[base + r], 1)], buf.at[1, pl.ds(r, 1)], sem.at[1]).start()
    h = h_ref[...]
    z = _silu(_dot(h, wsg_ref[...])) * _dot(h, wsu_ref[...])
    shared = _dot(z.astype(BF16), wsd_ref[...])
    for s in range(2):
        pltpu.make_async_copy(y_hbm.at[pl.ds(0, TM)], buf.at[s], sem.at[s]).wait()
    wt = wt_ref[...]
    fx = wt[:, 0:1] * buf[0] + wt[:, 1:2] * buf[1] + shared
    x = x_ref[...] + gate_ref[...] * (_rms(fx) * gain_ref[...])
    o_ref[...] = x
    if with_next:
        hn_ref[...] = _norm_mod(x, ngain_ref[...], nsc_ref[...], nsh_ref[...]).astype(BF16)


def _ffn_combine(h16, y, dest, wts, xs, wsg, wsu, wsd, gain, mods, n_lat, n_batch, nxt):
    b_sz, t_len, _ = xs.shape
    n_t = t_len // TM
    row = lambda b, t, d0, d1: (b, t, 0)
    const2 = lambda b, t, d0, d1: (0, 0)
    mod = lambda which: pl.BlockSpec(
        (None, None, 1, D),
        lambda b, t, d0, d1: (jnp.where(t >= n_lat, n_batch, b), which, 0, 0),
    )
    tile = pl.BlockSpec((None, TM, D), row)
    vec = pl.BlockSpec((1, D), const2)
    in_specs = [
        tile,
        pl.BlockSpec(wsg.shape, const2, pipeline_mode=pl.Buffered(1)),
        pl.BlockSpec(wsu.shape, const2, pipeline_mode=pl.Buffered(1)),
        pl.BlockSpec(wsd.shape, const2, pipeline_mode=pl.Buffered(1)),
        pl.BlockSpec(memory_space=pl.ANY),
        pl.BlockSpec((None, TM, 2), row),
        tile, mod(5), vec,
    ]
    args = [h16, wsg, wsu, wsd, y, wts.T.reshape(b_sz, t_len, 2), xs, mods, gain.reshape(1, D)]
    out_specs = [tile]
    out_shape = [jax.ShapeDtypeStruct((b_sz, t_len, D), F32)]
    if nxt is not None:
        in_specs += [vec, mod(1), mod(0)]
        args += [nxt[0].reshape(1, D), nxt[1], nxt[1]]
        out_specs.append(tile)
        out_shape.append(jax.ShapeDtypeStruct((b_sz, t_len, D), BF16))
    grid_spec = pltpu.PrefetchScalarGridSpec(
        num_scalar_prefetch=2,
        grid=(b_sz, n_t),
        in_specs=in_specs,
        out_specs=out_specs,
        scratch_shapes=[pltpu.VMEM((2, TM, D), F32), pltpu.SemaphoreType.DMA((2,))],
    )
    return pl.pallas_call(
        functools.partial(_ffn_kernel, n_t=n_t, with_next=nxt is not None),
        name="moe_shared_combine",
        grid_spec=grid_spec,
        out_shape=out_shape,
        compiler_params=_params(("arbitrary", "arbitrary")),
    )(dest[0], dest[1], *args)


def _moe_block(xs, h32, h16, gain, mods, n_lat, n_batch, wr_t, br, layer, wg, wu, wd,
               wsg, wsu, wsd, nxt):
    b_sz, t_len, _ = xs.shape
    rows = b_sz * t_len
    idx, wts = _router(h16.reshape(rows, D), wr_t, br)
    src, dest, tile_expert, tile_valid, n_tiles = _dispatch_plan(idx, rows)
    y = _gmm(h32.reshape(rows, D), src, tile_expert, tile_valid, n_tiles, layer, wg, wu, wd)
    return _ffn_combine(h16, y, dest, wts, xs, wsg, wsu, wsd, gain, mods, n_lat, n_batch, nxt)


def kernel(x, c, ctx, c_ctx, ada_w, ada_b, norm_g, w_in_e, w_out_e, na_rpb, wg_sink, w_in_o, mla_q_norm, mla_kv_norm, mla_w_qb, mla_w_kvb, w_out_o, w_router, b_router, moe_w_gate, moe_w_up, moe_w_down, shared_w_gate, shared_w_up, shared_w_down):
    b_sz, s_len, _ = x.shape
    c_len = ctx.shape[1]
    assert c_len == TM and s_len % MLA_TQ == 0 and b_sz < 8
    n_lat = s_len // TM
    bf = lambda a: a.astype(BF16)

    cc = jnp.zeros((8, D), F32).at[:b_sz].set(c).at[b_sz].set(c_ctx)
    mods = _ada_mods(cc, ada_w, ada_b)
    wr_t = bf(w_router.T)
    moe = lambda layer: (layer, moe_w_gate, moe_w_up, moe_w_down, bf(shared_w_gate[layer]),
                         bf(shared_w_up[layer]), bf(shared_w_down[layer]))

    xs, h = _norm_first(x, ctx, norm_g[0, 0], mods[0], b_sz)
    proj = _inproj_even(h, bf(w_in_e[0]), _rope_tables(s_len, c_len, HD))
    ya = _na_attention(proj, _na_bias_tables(na_rpb[0], s_len // GRID_W), s_len, c_len)
    yw = _wg_attention(proj, wg_sink[0], _wg_mask_tables(s_len), s_len, c_len)
    xs, h32, h16 = _outproj([ya, yw], bf(w_out_e[0]), xs, norm_g[0], mods[0], s_len + c_len, n_lat, b_sz)
    xs, h = _moe_block(xs, h32, h16, norm_g[0, 3], mods[0], n_lat, b_sz, wr_t, b_router, *moe(0),
                       (norm_g[1, 0], mods[1]))

    win = bf(jnp.pad(w_in_o[0], ((0, 0), (0, HD - MLA_ROPE))))
    wqb = bf(jnp.pad(mla_w_qb[0].reshape(MLA_RANK, MLA_HEADS, MLA_NOPE + MLA_ROPE),
                     ((0, 0), (0, 0), (0, MLA_QK - MLA_NOPE - MLA_ROPE))).reshape(MLA_RANK, -1))
    wkvb = mla_w_kvb[0].reshape(MLA_RANK, MLA_HEADS, MLA_NOPE + MLA_V)
    wkn = bf(wkvb[:, :, :MLA_NOPE].reshape(MLA_RANK, -1))
    wv = bf(wkvb[:, :, MLA_NOPE:].reshape(MLA_RANK, -1))
    q_t, k, v_t = _mla_in(h, win, wqb.T, wkn, wv.T, mla_q_norm[0], mla_kv_norm[0],
                          _rope_tables(s_len, c_len, MLA_ROPE))
    y = _mla_attention(q_t, k, v_t, s_len, c_len)
    xl, h32, h16 = _outproj([y], bf(w_out_o[0]), xs, norm_g[1], mods[1], s_len, n_lat, b_sz)
    (out,) = _moe_block(xl, h32, h16, norm_g[1, 3], mods[1], n_lat, b_sz, wr_t, b_router, *moe(1), None)
    return out
```

```python
import functools

import numpy as np
import jax
import jax.numpy as jnp
from jax import lax
from jax.experimental import pallas as pl
from jax.experimental.pallas import tpu as pltpu

F32 = jnp.float32
BF16 = jnp.bfloat16
I32 = jnp.int32

D = 2048
GRID_W = 64
HD = 128
EPS = 1e-6
NEG = -1e30
ROPE_THETA = 10000.0
LOG2E = 1.4426950408889634

NA_HEADS = 8
NA_WIN_ROWS = 8
NA_WIN_COLS = 16
NA_QROWS = 4
NA_KROWS = NA_QROWS + NA_WIN_ROWS
WG_HEADS = 8
WG_KV = 2
WG_GRP = WG_HEADS // WG_KV
WG_WINDOW = 128
WG_TQ = 256
WG_SPAN = WG_TQ + 2 * WG_WINDOW
EVEN_IN = (3 * NA_HEADS + WG_HEADS + 2 * WG_KV) * HD
MLA_HEADS = 16
MLA_RANK = 512
MLA_NOPE = 128
MLA_ROPE = 64
MLA_V = 128
MLA_QK = 256
MLA_TQ = 512
MLA_TK = 512
N_EXPERTS = 16
N_GROUPS = 4
GROUP_SZ = N_EXPERTS // N_GROUPS
D_EXPERT = 512
D_SHARED = 1024

TM = 256
TMG = 256
VMEM_LIMIT = 56 * 1024 * 1024


def _params(sem, vmem=VMEM_LIMIT):
    return pltpu.CompilerParams(dimension_semantics=sem, vmem_limit_bytes=vmem)


def _dot(a, b):
    return jnp.dot(a, b, preferred_element_type=F32)


def _dot_nt(a, b):
    return lax.dot_general(a, b, (((1,), (1,)), ((), ())), preferred_element_type=F32)


def _resident(shape):
    nd = len(shape)
    return pl.BlockSpec(shape, lambda *_: (0,) * nd, pipeline_mode=pl.Buffered(1))


def _silu(a):
    return a * jax.nn.sigmoid(a)


def _rms(x):
    return x * lax.rsqrt(jnp.mean(x * x, axis=-1, keepdims=True) + EPS)


def _ada_kernel(c_ref, w_ref, b_ref, o_ref):
    s = _silu(c_ref[...]).astype(BF16)
    o_ref[...] = _dot(s, w_ref[...].astype(BF16)) + b_ref[...]


def _ada_mods(cc, ada_w, ada_b):
    n_layers, _, n_out = ada_w.shape
    tn = 1024
    out = pl.pallas_call(
        _ada_kernel,
        name="ada_mods",
        grid=(n_layers, n_out // tn),
        in_specs=[
            pl.BlockSpec((8, D), lambda l, j: (0, 0)),
            pl.BlockSpec((None, D, tn), lambda l, j: (l, 0, j)),
            pl.BlockSpec((None, 1, tn), lambda l, j: (l, 0, j)),
        ],
        out_specs=pl.BlockSpec((None, 8, tn), lambda l, j: (l, 0, j)),
        out_shape=jax.ShapeDtypeStruct((n_layers, 8, n_out), F32),
        compiler_params=_params(("parallel", "parallel")),
    )(cc, ada_w, ada_b.reshape(n_layers, 1, n_out))
    return out.reshape(n_layers, 8, 6, 1, D)


def _mod_spec(which, n_lat_tiles, ctx_row):
    return pl.BlockSpec(
        (None, None, 1, D),
        lambda b, t: (jnp.where(t >= n_lat_tiles, ctx_row, b), which, 0, 0),
    )


def _norm_mod(x, g, sc, sh):
    return (_rms(x) * g) * (1.0 + sc) + sh


def _norm_first_kernel(x_ref, c_ref, g_ref, sc_ref, sh_ref, xs_ref, h_ref, *, n_lat):
    t = pl.program_id(1)

    def emit(src):
        x = src[...]
        xs_ref[...] = x
        h_ref[...] = _norm_mod(x, g_ref[...], sc_ref[...], sh_ref[...]).astype(BF16)

    @pl.when(t < n_lat)
    def _():
        emit(x_ref)

    @pl.when(t == n_lat)
    def _():
        emit(c_ref)


def _norm_first(x, ctx, gain, mods, n_batch):
    b_sz, s_len, _ = x.shape
    c_len = ctx.shape[1]
    n_lat = s_len // TM
    t_len = s_len + c_len
    row = lambda b, t: (b, t, 0)
    return pl.pallas_call(
        functools.partial(_norm_first_kernel, n_lat=n_lat),
        name="norm_first",
        grid=(b_sz, n_lat + 1),
        in_specs=[
            pl.BlockSpec((None, TM, D), lambda b, t: (b, jnp.minimum(t, n_lat - 1), 0)),
            pl.BlockSpec((None, TM, D), lambda b, t: (b, 0, 0)),
            pl.BlockSpec((1, D), lambda b, t: (0, 0)),
            _mod_spec(1, n_lat, n_batch),
            _mod_spec(0, n_lat, n_batch),
        ],
        out_specs=[pl.BlockSpec((None, TM, D), row), pl.BlockSpec((None, TM, D), row)],
        out_shape=[
            jax.ShapeDtypeStruct((b_sz, t_len, D), F32),
            jax.ShapeDtypeStruct((b_sz, t_len, D), BF16),
        ],
        compiler_params=_params(("parallel", "arbitrary")),
    )(x, ctx, gain.reshape(1, D), mods, mods)


def _rope_tables(s_len, c_len, rot_dim):
    quarter = rot_dim // 4
    t = jnp.arange(s_len, dtype=I32)
    row = (t // GRID_W).astype(F32)
    col = (t % GRID_W).astype(F32)
    inv_freq = ROPE_THETA ** (-jnp.arange(quarter, dtype=F32) / quarter)
    ang_r = row[:, None] * inv_freq[None, :]
    ang_c = col[:, None] * inv_freq[None, :]
    ang = jnp.concatenate([ang_r, ang_r, ang_c, ang_c], axis=-1)
    cos, sin = jnp.cos(ang), jnp.sin(ang)
    even = ((jnp.arange(rot_dim) // quarter) % 2 == 0)[None, :]
    sa = jnp.where(even, -sin, 0.0)
    sb = jnp.where(even, 0.0, sin)
    pad = ((0, c_len), (0, HD - rot_dim))
    return (
        jnp.pad(cos, pad, constant_values=1.0),
        jnp.pad(sa, pad),
        jnp.pad(sb, pad),
    )


def _rope(x, cos, sa, sb, quarter):
    return x * cos + pltpu.roll(x, HD - quarter, 1) * sa + pltpu.roll(x, quarter, 1) * sb


EVEN_TN = 1536
EVEN_ROPE_BLOCK = 2
EVEN_ROPE_CHUNKS = WG_HEADS + WG_KV


ATTN_QSCALE = HD ** -0.5 * LOG2E


def _inproj_even_kernel(h_ref, w_ref, cos_ref, sa_ref, sb_ref, o_ref):
    j = pl.program_id(0)
    acc = _dot(h_ref[...], w_ref[...])
    n_q = NA_HEADS * HD

    @pl.when(j == 0)
    def _():
        o_ref[:, :n_q] = (acc[:, :n_q] * ATTN_QSCALE).astype(BF16)
        o_ref[:, n_q:] = acc[:, n_q:].astype(BF16)

    @pl.when((j != 0) & (j != EVEN_ROPE_BLOCK))
    def _():
        o_ref[...] = acc.astype(BF16)

    @pl.when(j == EVEN_ROPE_BLOCK)
    def _():
        cos, sa, sb = cos_ref[...], sa_ref[...], sb_ref[...]
        for ch in range(EVEN_ROPE_CHUNKS):
            sl = slice(ch * HD, (ch + 1) * HD)
            r = _rope(acc[:, sl], cos, sa, sb, HD // 4)
            o_ref[:, sl] = (r * ATTN_QSCALE if ch < WG_HEADS else r).astype(BF16)
        rest = slice(EVEN_ROPE_CHUNKS * HD, EVEN_TN)
        o_ref[:, rest] = acc[:, rest].astype(BF16)


def _inproj_even(h, w, tables):
    b_sz, t_len, _ = h.shape
    n_t = t_len // TM
    rows = b_sz * t_len
    tab = pl.BlockSpec((TM, HD), lambda j, m: (m % n_t, 0))
    out = pl.pallas_call(
        _inproj_even_kernel,
        name="inproj_even",
        grid=(EVEN_IN // EVEN_TN, rows // TM),
        in_specs=[
            pl.BlockSpec((TM, D), lambda j, m: (m, 0)),
            pl.BlockSpec((D, EVEN_TN), lambda j, m: (0, j)),
            tab, tab, tab,
        ],
        out_specs=pl.BlockSpec((TM, EVEN_TN), lambda j, m: (m, j)),
        out_shape=jax.ShapeDtypeStruct((rows, EVEN_IN), BF16),
        compiler_params=_params(("parallel", "parallel")),
    )(h.reshape(rows, D), w, *tables)
    return out.reshape(b_sz, t_len, EVEN_IN)


def _na_bias_tables(rpb, rows):
    n_heads, n_dr, n_dc = rpb.shape
    w = GRID_W
    centre = NA_WIN_COLS - 1
    vec = jnp.pad(rpb.astype(F32) * LOG2E, ((0, 0), (0, 0), (w - 1 - centre, w - n_dc + centre + 1)))
    toe = jnp.tile(vec, (1, 1, w))[:, :, :w * (2 * w - 1)].reshape(n_heads, n_dr, w, 2 * w - 1)
    toe = toe[:, :, :, w - 1:]
    qc = np.arange(w)
    c_start = np.clip(qc - NA_WIN_COLS // 2, 0, w - NA_WIN_COLS)
    col_ok = (qc[None, :] >= c_start[:, None]) & (qc[None, :] < c_start[:, None] + NA_WIN_COLS)
    n_tiles = rows // NA_QROWS
    tabs = []
    for i in (0, 1, n_tiles - 1):
        r0 = min(max(NA_QROWS * i - NA_WIN_ROWS // 2, 0), rows - NA_KROWS)
        blocks = []
        for qr in range(NA_QROWS * i, NA_QROWS * (i + 1)):
            r_start = min(max(qr - NA_WIN_ROWS // 2, 0), rows - NA_WIN_ROWS)
            per_kr = []
            for kr in range(r0, r0 + NA_KROWS):
                if r_start <= kr < r_start + NA_WIN_ROWS:
                    per_kr.append(jnp.where(col_ok[None], toe[:, kr - qr + NA_WIN_ROWS - 1], NEG))
                else:
                    per_kr.append(jnp.full((n_heads, w, w), NEG, F32))
            blocks.append(jnp.concatenate(per_kr, axis=-1))
        tabs.append(jnp.concatenate(blocks, axis=1))
    return jnp.stack(tabs, axis=1)


def _softmax_pv(parts, extra_logit=None):
    m = parts[0][0].max(axis=-1, keepdims=True)
    for s, _ in parts[1:]:
        m = jnp.maximum(m, s.max(axis=-1, keepdims=True))
    if extra_logit is not None:
        m = jnp.maximum(m, extra_logit)
    l = None
    o = None
    for s, v in parts:
        p = jnp.exp2(s - m)
        ls = p.sum(axis=-1, keepdims=True)
        os_ = _dot(p.astype(BF16), v)
        l = ls if l is None else l + ls
        o = os_ if o is None else o + os_
    if extra_logit is not None:
        l = l + jnp.exp2(extra_logit - m)
    return o / l


def _na_kernel(q_ref, k_ref, v_ref, bias_ref, o_ref, *, s_len, c_len):
    rows = s_len // GRID_W
    tq = NA_QROWS * GRID_W
    tk = NA_KROWS * GRID_W
    n_q = s_len // tq
    kc = k_ref[s_len:s_len + c_len, :]
    vc = v_ref[s_len:s_len + c_len, :]

    def body(i, carry):
        qs = pl.multiple_of(i * tq, tq)
        r0 = jnp.clip(NA_QROWS * i - NA_WIN_ROWS // 2, 0, rows - NA_KROWS)
        ks = pl.multiple_of(r0 * GRID_W, tq)
        typ = jnp.where(i == 0, 0, jnp.where(i == n_q - 1, 2, 1))
        q = q_ref[pl.ds(qs, tq), :]
        s_loc = _dot_nt(q, k_ref[pl.ds(ks, tk), :]) + bias_ref[typ]
        s_ctx = _dot_nt(q, kc)
        o = _softmax_pv([(s_loc, v_ref[pl.ds(ks, tk), :]), (s_ctx, vc)])
        o_ref[pl.ds(qs, tq), :] = o.astype(BF16)
        return carry

    lax.fori_loop(0, n_q, body, 0)
    s_cc = _dot_nt(q_ref[s_len:s_len + c_len, :], kc)
    o_ref[s_len:s_len + c_len, :] = _softmax_pv([(s_cc, vc)]).astype(BF16)


def _na_attention(proj, bias, s_len, c_len):
    b_sz, t_len, _ = proj.shape
    col = lambda off: pl.BlockSpec((None, t_len, HD), lambda b, h: (b, 0, off + h))
    return pl.pallas_call(
        functools.partial(_na_kernel, s_len=s_len, c_len=c_len),
        name="na_attention",
        grid=(b_sz, NA_HEADS),
        in_specs=[
            col(0), col(NA_HEADS), col(2 * NA_HEADS),
            pl.BlockSpec((None,) + bias.shape[1:], lambda b, h: (h, 0, 0, 0)),
        ],
        out_specs=pl.BlockSpec((None, t_len, HD), lambda b, h: (b, 0, h)),
        out_shape=jax.ShapeDtypeStruct((b_sz, t_len, NA_HEADS * HD), BF16),
        compiler_params=_params(("parallel", "parallel")),
    )(proj, proj, proj, bias)


def _wg_mask_tables(s_len):
    n_q = s_len // WG_TQ
    tabs = []
    for i in (0, 1, n_q - 1):
        k0 = min(max(WG_TQ * i - WG_WINDOW, 0), s_len - WG_SPAN)
        rel = (k0 + jnp.arange(WG_SPAN))[None, :] - (WG_TQ * i + jnp.arange(WG_TQ))[:, None]
        tabs.append(jnp.where(jnp.abs(rel) <= WG_WINDOW, 0.0, NEG).astype(F32))
    return jnp.stack(tabs)


def _wg_kernel(sink_ref, q_ref, k_ref, v_ref, mask_ref, o_ref, *, s_len, c_len):
    kvh = pl.program_id(1)
    n_q = s_len // WG_TQ
    kc = k_ref[s_len:s_len + c_len, :]
    vc = v_ref[s_len:s_len + c_len, :]
    sinks = [sink_ref[kvh * WG_GRP + g] * LOG2E for g in range(WG_GRP)]

    def body(i, carry):
        qs = pl.multiple_of(i * WG_TQ, WG_TQ)
        ks = pl.multiple_of(jnp.clip(WG_TQ * i - WG_WINDOW, 0, s_len - WG_SPAN), WG_WINDOW)
        typ = jnp.where(i == 0, 0, jnp.where(i == n_q - 1, 2, 1))
        kl = k_ref[pl.ds(ks, WG_SPAN), :]
        vl = v_ref[pl.ds(ks, WG_SPAN), :]
        mask = mask_ref[typ]
        for g in range(WG_GRP):
            sl = slice(g * HD, (g + 1) * HD)
            q = q_ref[pl.ds(qs, WG_TQ), sl]
            o = _softmax_pv([(_dot_nt(q, kl) + mask, vl), (_dot_nt(q, kc), vc)], sinks[g])
            o_ref[pl.ds(qs, WG_TQ), sl] = o.astype(BF16)
        return carry

    lax.fori_loop(0, n_q, body, 0)
    for g in range(WG_GRP):
        sl = slice(g * HD, (g + 1) * HD)
        o = _softmax_pv([(_dot_nt(q_ref[s_len:s_len + c_len, sl], kc), vc)], sinks[g])
        o_ref[s_len:s_len + c_len, sl] = o.astype(BF16)


def _wg_attention(proj, sink, mask, s_len, c_len):
    b_sz, t_len, _ = proj.shape
    q_off = 3 * NA_HEADS * HD // (WG_GRP * HD)
    k_off = 3 * NA_HEADS + WG_HEADS
    v_off = k_off + WG_KV
    return pl.pallas_call(
        functools.partial(_wg_kernel, s_len=s_len, c_len=c_len),
        name="wg_attention",
        grid=(b_sz, WG_KV),
        in_specs=[
            pl.BlockSpec(memory_space=pltpu.SMEM),
            pl.BlockSpec((None, t_len, WG_GRP * HD), lambda b, h: (b, 0, q_off + h)),
            pl.BlockSpec((None, t_len, HD), lambda b, h: (b, 0, k_off + h)),
            pl.BlockSpec((None, t_len, HD), lambda b, h: (b, 0, v_off + h)),
            pl.BlockSpec(mask.shape, lambda b, h: (0, 0, 0)),
        ],
        out_specs=pl.BlockSpec((None, t_len, WG_GRP * HD), lambda b, h: (b, 0, h)),
        out_shape=jax.ShapeDtypeStruct((b_sz, t_len, WG_HEADS * HD), BF16),
        compiler_params=_params(("parallel", "parallel")),
    )(sink.astype(F32), proj, proj, proj, mask)


def _outproj_kernel(*refs, n_parts):
    y_refs = refs[:n_parts]
    (w_ref, x_ref, gate_ref, gain_ref, ngain_ref, nsc_ref, nsh_ref,
     o_ref, h32_ref, h16_ref) = refs[n_parts:]
    y = None
    off = 0
    for y_ref in y_refs:
        k = y_ref.shape[-1]
        part = _dot(y_ref[...], w_ref[off:off + k, :])
        y = part if y is None else y + part
        off += k
    x = x_ref[...] + gate_ref[...] * (_rms(y) * gain_ref[...])
    o_ref[...] = x
    h = _norm_mod(x, ngain_ref[...], nsc_ref[...], nsh_ref[...])
    h32_ref[...] = h
    h16_ref[...] = h.astype(BF16)


def _outproj(ys, w, xs, gains, mods, n_rows, n_lat, n_batch):
    b_sz = xs.shape[0]
    row = lambda b, t: (b, t, 0)
    vec = pl.BlockSpec((1, D), lambda b, t: (0, 0))
    tile = pl.BlockSpec((None, TM, D), row)
    return pl.pallas_call(
        functools.partial(_outproj_kernel, n_parts=len(ys)),
        name="outproj_residual",
        grid=(b_sz, n_rows // TM),
        in_specs=[pl.BlockSpec((None, TM, y.shape[-1]), row) for y in ys] + [
            _resident(w.shape), tile, _mod_spec(2, n_lat, n_batch), vec,
            vec, _mod_spec(4, n_lat, n_batch), _mod_spec(3, n_lat, n_batch),
        ],
        out_specs=[tile, tile, tile],
        out_shape=[
            jax.ShapeDtypeStruct((b_sz, n_rows, D), F32),
            jax.ShapeDtypeStruct((b_sz, n_rows, D), F32),
            jax.ShapeDtypeStruct((b_sz, n_rows, D), BF16),
        ],
        compiler_params=_params(("parallel", "parallel")),
    )(*ys, w, xs, mods, gains[1].reshape(1, D), gains[2].reshape(1, D), mods, mods)


def _roll_rows(x, shift):
    return jnp.concatenate([x[-shift:], x[:-shift]], axis=0)


def _mla_in_kernel(h_ref, win_ref, wqbt_ref, wkn_ref, wvt_ref, qg_ref, kvg_ref,
                   cos_ref, sa_ref, sb_ref, cost_ref, sat_ref, sbt_ref, q_ref, k_ref, v_ref):
    quarter = MLA_ROPE // 4
    c = _dot(h_ref[...], win_ref[...])
    cq = c[:, :MLA_RANK]
    ckv = c[:, MLA_RANK:2 * MLA_RANK]
    k_rope = _rope(c[:, 2 * MLA_RANK:], cos_ref[...], sa_ref[...], sb_ref[...], quarter).astype(BF16)
    q_scale = (MLA_NOPE + MLA_ROPE) ** -0.5 * LOG2E
    q_t = _dot_nt(wqbt_ref[...], (_rms(cq) * qg_ref[...]).astype(BF16))
    kvn = (_rms(ckv) * kvg_ref[...]).astype(BF16)
    k_nope = _dot(kvn, wkn_ref[...])
    v_t = _dot_nt(wvt_ref[...], kvn)
    cos_t, sa_t, sb_t = cost_ref[...], sat_ref[...], sbt_ref[...]
    for h in range(MLA_HEADS):
        lo = slice(h * MLA_QK, h * MLA_QK + HD)
        hi = slice(h * MLA_QK + HD, (h + 1) * MLA_QK)
        x = q_t[hi]
        x = x * cos_t + _roll_rows(x, HD - quarter) * sa_t + _roll_rows(x, quarter) * sb_t
        q_ref[lo, :] = (q_t[lo] * q_scale).astype(BF16)
        q_ref[hi, :] = (x * q_scale).astype(BF16)
        k_ref[:, lo] = k_nope[:, h * HD:(h + 1) * HD].astype(BF16)
        k_ref[:, hi] = k_rope
        v_ref[h] = v_t[h * MLA_V:(h + 1) * MLA_V].astype(BF16)


def _mla_in(h, win, wqbt, wkn, wvt, q_gain, kv_gain, tables):
    b_sz, t_len, _ = h.shape
    n_t = t_len // TM
    tab = pl.BlockSpec((TM, HD), lambda b, t: (t, 0))
    tab_t = pl.BlockSpec((HD, TM), lambda b, t: (0, t))
    tables_t = tuple(a.T for a in tables)
    return pl.pallas_call(
        _mla_in_kernel,
        name="mla_in",
        grid=(b_sz, n_t),
        in_specs=[
            pl.BlockSpec((None, TM, D), lambda b, t: (b, t, 0)),
            _resident(win.shape), _resident(wqbt.shape), _resident(wkn.shape),
            _resident(wvt.shape), _resident((1, MLA_RANK)), _resident((1, MLA_RANK)),
            tab, tab, tab, tab_t, tab_t, tab_t,
        ],
        out_specs=[
            pl.BlockSpec((None, MLA_HEADS * MLA_QK, TM), lambda b, t: (b, 0, t)),
            pl.BlockSpec((None, TM, MLA_HEADS * MLA_QK), lambda b, t: (b, t, 0)),
            pl.BlockSpec((None, None, MLA_HEADS, MLA_V, TM), lambda b, t: (b, t, 0, 0, 0)),
        ],
        out_shape=[
            jax.ShapeDtypeStruct((b_sz, MLA_HEADS * MLA_QK, t_len), BF16),
            jax.ShapeDtypeStruct((b_sz, t_len, MLA_HEADS * MLA_QK), BF16),
            jax.ShapeDtypeStruct((b_sz, n_t, MLA_HEADS, MLA_V, TM), BF16),
        ],
        compiler_params=_params(("parallel", "parallel")),
    )(h, win, wqbt, wkn, wvt, q_gain.reshape(1, MLA_RANK), kv_gain.reshape(1, MLA_RANK),
      *tables, *tables_t)


MLA_SUB = 64
MLA_SLOTS = 3


def _mla_attn_kernel(q_ref, k_ref, v_ref, o_ref, s_scr, p_scr, *, n_tiles):
    q_t = q_ref[...]

    def logits(tile, slot):
        ks = pl.multiple_of(tile * TM, TM)
        s = _dot(k_ref[pl.ds(ks, TM), :], q_t)
        s_scr[slot] = s
        return s.max(axis=0, keepdims=True)

    def softmax(s_slot, p_slot, m_tile, m, l):
        m_new = jnp.maximum(m, m_tile)
        alpha = jnp.exp2(m - m_new)
        lsum = jnp.zeros((8, MLA_TQ), F32)
        for r in range(0, TM, MLA_SUB):
            p = jnp.exp2(s_scr[s_slot, r:r + MLA_SUB, :] - m_new)
            for g in range(0, MLA_SUB, 8):
                lsum = lsum + p[g:g + 8]
            p_scr[p_slot, r:r + MLA_SUB, :] = p.astype(BF16)
        return m_new, alpha, alpha * l + lsum.sum(axis=0, keepdims=True)

    m = jnp.full((1, MLA_TQ), -jnp.inf, F32)
    l = jnp.zeros((1, MLA_TQ), F32)
    acc = jnp.zeros((MLA_V, MLA_TQ), F32)
    alpha_prev = None
    m_tile = logits(0, 0)
    for t in range(n_tiles):
        m_next = logits(t + 1, (t + 1) % MLA_SLOTS) if t + 1 < n_tiles else None
        if t > 0:
            pv_prev = _dot(v_ref[t - 1], p_scr[(t - 1) % 2])
        m, alpha, l = softmax(t % MLA_SLOTS, t % 2, m_tile, m, l)
        if t > 0:
            acc = alpha_prev * acc + pv_prev
        alpha_prev, m_tile = alpha, m_next
    acc = alpha_prev * acc + _dot(v_ref[n_tiles - 1], p_scr[(n_tiles - 1) % 2])
    o_ref[...] = (acc / l).T.astype(BF16)


def _mla_attention(q_t, k, v_t, s_len, c_len):
    b_sz, t_len, _ = k.shape
    n_t = t_len // TM
    return pl.pallas_call(
        functools.partial(_mla_attn_kernel, n_tiles=n_t),
        name="mla_attention",
        grid=(b_sz, MLA_HEADS, s_len // MLA_TQ),
        in_specs=[
            pl.BlockSpec((None, MLA_QK, MLA_TQ), lambda b, h, i: (b, h, i)),
            pl.BlockSpec((None, t_len, MLA_QK), lambda b, h, i: (b, 0, h)),
            pl.BlockSpec((None, n_t, None, MLA_V, TM), lambda b, h, i: (b, 0, h, 0, 0)),
        ],
        out_specs=pl.BlockSpec((None, MLA_TQ, MLA_V), lambda b, h, i: (b, i, h)),
        out_shape=jax.ShapeDtypeStruct((b_sz, s_len, MLA_HEADS * MLA_V), BF16),
        scratch_shapes=[pltpu.VMEM((MLA_SLOTS, TM, MLA_TQ), F32), pltpu.VMEM((2, TM, MLA_TQ), BF16)],
        compiler_params=_params(("parallel", "parallel", "parallel")),
    )(q_t, k, v_t)


def _router_kernel(h_ref, wr_ref, br_ref, idx_ref, wt_ref):
    scores = jax.nn.sigmoid(_dot_nt(wr_ref[...], h_ref[...]))
    sel = scores + br_ref[...]
    sel_rows = [sel[e:e + 1, :] for e in range(N_EXPERTS)]
    score_rows = [scores[e:e + 1, :] for e in range(N_EXPERTS)]

    group_scores = []
    for g in range(N_GROUPS):
        a, b, c, d = sel_rows[g * GROUP_SZ:(g + 1) * GROUP_SZ]
        hi1, lo1 = jnp.maximum(a, b), jnp.minimum(a, b)
        hi2, lo2 = jnp.maximum(c, d), jnp.minimum(c, d)
        top1 = jnp.maximum(hi1, hi2)
        top2 = jnp.maximum(jnp.minimum(hi1, hi2), jnp.maximum(lo1, lo2))
        group_scores.append(top1 + top2)
    best = group_scores[0]
    gi = jnp.zeros_like(best, dtype=I32)
    for g in range(1, N_GROUPS):
        upd = group_scores[g] > best
        gi = jnp.where(upd, g, gi)
        best = jnp.where(upd, group_scores[g], best)

    def pick_group(rows_, j):
        out = rows_[j]
        for g in range(1, N_GROUPS):
            out = jnp.where(gi == g, rows_[g * GROUP_SZ + j], out)
        return out

    gsel = [pick_group(sel_rows, j) for j in range(GROUP_SZ)]
    gscore = [pick_group(score_rows, j) for j in range(GROUP_SZ)]

    def argmax_excluding(skip):
        val = jnp.full_like(best, -jnp.inf)
        idx = jnp.zeros_like(gi)
        wt = jnp.zeros_like(best)
        for j in range(GROUP_SZ):
            cand = gsel[j] if skip is None else jnp.where(skip == j, -jnp.inf, gsel[j])
            upd = cand > val
            idx = jnp.where(upd, j, idx)
            wt = jnp.where(upd, gscore[j], wt)
            val = jnp.where(upd, cand, val)
        return idx, wt

    i1, w1 = argmax_excluding(None)
    i2, w2 = argmax_excluding(i1)
    tot = w1 + w2
    idx_ref[0:1, :] = gi * GROUP_SZ + i1
    idx_ref[1:2, :] = gi * GROUP_SZ + i2
    wt_ref[0:1, :] = w1 / tot
    wt_ref[1:2, :] = w2 / tot


def _router(h16, wr_t, br):
    rows = h16.shape[0]
    return pl.pallas_call(
        _router_kernel,
        name="moe_router",
        grid=(rows // TM,),
        in_specs=[
            pl.BlockSpec((TM, D), lambda m: (m, 0)),
            pl.BlockSpec((N_EXPERTS, D), lambda m: (0, 0)),
            pl.BlockSpec((N_EXPERTS, 1), lambda m: (0, 0)),
        ],
        out_specs=[pl.BlockSpec((2, TM), lambda m: (0, m)), pl.BlockSpec((2, TM), lambda m: (0, m))],
        out_shape=[jax.ShapeDtypeStruct((2, rows), I32), jax.ShapeDtypeStruct((2, rows), F32)],
        compiler_params=_params(("parallel",)),
    )(h16, wr_t, br.reshape(N_EXPERTS, 1).astype(F32))


def _dispatch_plan(idx, rows):
    n_tiles = 2 * rows // TMG + N_EXPERTS
    e = idx.reshape(-1)
    onehot = (e[:, None] == jnp.arange(N_EXPERTS, dtype=I32)[None, :]).astype(I32)
    csum = jnp.cumsum(onehot, axis=0)
    rank = jnp.take_along_axis(csum, e[:, None], axis=1)[:, 0] - 1
    counts = csum[-1]
    padded = (counts + TMG - 1) // TMG * TMG
    ends = jnp.cumsum(padded)
    dest = (ends - padded)[e] + rank
    tok = jnp.tile(jnp.arange(rows, dtype=I32), 2)
    src = jnp.zeros((n_tiles * TMG,), I32).at[dest].set(tok)
    tile_start = jnp.arange(n_tiles, dtype=I32) * TMG
    tile_expert = jnp.minimum(
        jnp.sum((ends[None, :] <= tile_start[:, None]).astype(I32), axis=1), N_EXPERTS - 1)
    tile_valid = (tile_start < ends[-1]).astype(I32)
    return src, dest.reshape(2, rows).astype(I32), tile_expert, tile_valid, n_tiles


def _gmm_kernel(te_ref, tv_ref, src_ref, h_hbm, wg_ref, wu_ref, wd_ref, y_ref,
                buf0, buf1, wg_s, wu_s, wd_s, sem):
    i = pl.program_id(0)
    prev = jnp.maximum(i - 1, 0)
    bufs = (buf0, buf1)

    def gather(tile, slot):
        base = tile * TMG
        for r in range(TMG):
            pltpu.make_async_copy(
                h_hbm.at[pl.ds(src_ref[base + r], 1)], bufs[slot].at[pl.ds(r, 1)], sem.at[slot]
            ).start(priority=r % 2)

    def wait(slot):
        pltpu.make_async_copy(h_hbm.at[pl.ds(0, TMG)], bufs[slot], sem.at[slot]).wait()

    @pl.when(i == 0)
    def _():
        gather(0, 0)

    @pl.when((i == 0) | (te_ref[i] != te_ref[prev]))
    def _():
        wg_s[...] = wg_ref[...].astype(BF16)
        wu_s[...] = wu_ref[...].astype(BF16)
        wd_s[...] = wd_ref[...].astype(BF16)

    def compute(slot):
        wait(slot)
        x = bufs[slot][...].astype(BF16)
        gather(i + 1, 1 - slot)
        z = _silu(_dot(x, wg_s[...])) * _dot(x, wu_s[...])
        y_ref[...] = _dot(z.astype(BF16), wd_s[...])

    for parity in range(2):
        pl.when((tv_ref[i] > 0) & (i % 2 == parity))(functools.partial(compute, parity))

    @pl.when(tv_ref[i] == 0)
    def _():
        y_ref[...] = jnp.zeros_like(y_ref)

    for parity in range(2):
        pl.when((tv_ref[i] == 0) & (tv_ref[prev] > 0) & (i % 2 == parity))(
            functools.partial(wait, parity))


def _gmm(h32, src, tile_expert, tile_valid, n_tiles, layer, wg, wu, wd):
    expert = lambda i, te, tv, src_: (layer, te[i], 0, 0)
    grid_spec = pltpu.PrefetchScalarGridSpec(
        num_scalar_prefetch=3,
        grid=(n_tiles,),
        in_specs=[
            pl.BlockSpec(memory_space=pl.ANY),
            pl.BlockSpec((None, None, D, D_EXPERT), expert),
            pl.BlockSpec((None, None, D, D_EXPERT), expert),
            pl.BlockSpec((None, None, D_EXPERT, D), expert),
        ],
        out_specs=pl.BlockSpec((TMG, D), lambda i, te, tv, src_: (i, 0)),
        scratch_shapes=[
            pltpu.VMEM((TMG, D), F32),
            pltpu.VMEM((TMG, D), F32),
            pltpu.VMEM((D, D_EXPERT), BF16),
            pltpu.VMEM((D, D_EXPERT), BF16),
            pltpu.VMEM((D_EXPERT, D), BF16),
            pltpu.SemaphoreType.DMA((2,)),
        ],
    )
    return pl.pallas_call(
        _gmm_kernel,
        name="moe_experts",
        grid_spec=grid_spec,
        out_shape=jax.ShapeDtypeStruct((n_tiles * TMG, D), F32),
        compiler_params=_params(("arbitrary",)),
    )(tile_expert, tile_valid, src, h32, wg, wu, wd)


def _ffn_kernel(d0_ref, d1_ref, h_ref, wsg_ref, wsu_ref, wsd_ref, y_hbm, wt_ref, x_ref,
                gate_ref, gain_ref, *rest, n_t, with_next):
    if with_next:
        ngain_ref, nsc_ref, nsh_ref, o_ref, hn_ref, buf, sem = rest
    else:
        o_ref, buf, sem = rest
    base = (pl.program_id(0) * n_t + pl.program_id(1)) * TM
    for r in range(TM):
        pltpu.make_async_copy(y_hbm.at[pl.ds(d0_ref[base + r], 1)], buf.at[0, pl.ds(r, 1)], sem.at[0]).start(priority=0)
        pltpu.make_async_copy(y_hbm.at[pl.ds(d1_ref[base + r], 1)], buf.at[1, pl.ds(r, 1)], sem.at[1]).start(priority=1)
    h = h_ref[...]
    z = _silu(_dot(h, wsg_ref[...])) * _dot(h, wsu_ref[...])
    shared = _dot(z.astype(BF16), wsd_ref[...])
    for s in range(2):
        pltpu.make_async_copy(y_hbm.at[pl.ds(0, TM)], buf.at[s], sem.at[s]).wait()
    wt = wt_ref[...]
    fx = wt[:, 0:1] * buf[0] + wt[:, 1:2] * buf[1] + shared
    x = x_ref[...] + gate_ref[...] * (_rms(fx) * gain_ref[...])
    o_ref[...] = x
    if with_next:
        hn_ref[...] = _norm_mod(x, ngain_ref[...], nsc_ref[...], nsh_ref[...]).astype(BF16)


def _ffn_combine(h16, y, dest, wts, xs, wsg, wsu, wsd, gain, mods, n_lat, n_batch, nxt):
    b_sz, t_len, _ = xs.shape
    n_t = t_len // TM
    row = lambda b, t, d0, d1: (b, t, 0)
    const2 = lambda b, t, d0, d1: (0, 0)
    mod = lambda which: pl.BlockSpec(
        (None, None, 1, D),
        lambda b, t, d0, d1: (jnp.where(t >= n_lat, n_batch, b), which, 0, 0),
    )
    tile = pl.BlockSpec((None, TM, D), row)
    vec = pl.BlockSpec((1, D), const2)
    in_specs = [
        tile,
        pl.BlockSpec(wsg.shape, const2, pipeline_mode=pl.Buffered(1)),
        pl.BlockSpec(wsu.shape, const2, pipeline_mode=pl.Buffered(1)),
        pl.BlockSpec(wsd.shape, const2, pipeline_mode=pl.Buffered(1)),
        pl.BlockSpec(memory_space=pl.ANY),
        pl.BlockSpec((None, TM, 2), row),
        tile, mod(5), vec,
    ]
    args = [h16, wsg, wsu, wsd, y, wts.T.reshape(b_sz, t_len, 2), xs, mods, gain.reshape(1, D)]
    out_specs = [tile]
    out_shape = [jax.ShapeDtypeStruct((b_sz, t_len, D), F32)]
    if nxt is not None:
        in_specs += [vec, mod(1), mod(0)]
        args += [nxt[0].reshape(1, D), nxt[1], nxt[1]]
        out_specs.append(tile)
        out_shape.append(jax.ShapeDtypeStruct((b_sz, t_len, D), BF16))
    grid_spec = pltpu.PrefetchScalarGridSpec(
        num_scalar_prefetch=2,
        grid=(b_sz, n_t),
        in_specs=in_specs,
        out_specs=out_specs,
        scratch_shapes=[pltpu.VMEM((2, TM, D), F32), pltpu.SemaphoreType.DMA((2,))],
    )
    return pl.pallas_call(
        functools.partial(_ffn_kernel, n_t=n_t, with_next=nxt is not None),
        name="moe_shared_combine",
        grid_spec=grid_spec,
        out_shape=out_shape,
        compiler_params=_params(("arbitrary", "arbitrary")),
    )(dest[0], dest[1], *args)


def _moe_block(xs, h32, h16, gain, mods, n_lat, n_batch, wr_t, br, layer, wg, wu, wd,
               wsg, wsu, wsd, nxt):
    b_sz, t_len, _ = xs.shape
    rows = b_sz * t_len
    idx, wts = _router(h16.reshape(rows, D), wr_t, br)
    src, dest, tile_expert, tile_valid, n_tiles = _dispatch_plan(idx, rows)
    y = _gmm(h32.reshape(rows, D), src, tile_expert, tile_valid, n_tiles, layer, wg, wu, wd)
    return _ffn_combine(h16, y, dest, wts, xs, wsg, wsu, wsd, gain, mods, n_lat, n_batch, nxt)


def kernel(x, c, ctx, c_ctx, ada_w, ada_b, norm_g, w_in_e, w_out_e, na_rpb, wg_sink, w_in_o, mla_q_norm, mla_kv_norm, mla_w_qb, mla_w_kvb, w_out_o, w_router, b_router, moe_w_gate, moe_w_up, moe_w_down, shared_w_gate, shared_w_up, shared_w_down):
    b_sz, s_len, _ = x.shape
    c_len = ctx.shape[1]
    assert c_len == TM and s_len % MLA_TQ == 0 and b_sz < 8
    n_lat = s_len // TM
    bf = lambda a: a.astype(BF16)

    cc = jnp.zeros((8, D), F32).at[:b_sz].set(c).at[b_sz].set(c_ctx)
    mods = _ada_mods(cc, ada_w, ada_b)
    wr_t = bf(w_router.T)
    moe = lambda layer: (layer, moe_w_gate, moe_w_up, moe_w_down, bf(shared_w_gate[layer]),
                         bf(shared_w_up[layer]), bf(shared_w_down[layer]))

    xs, h = _norm_first(x, ctx, norm_g[0, 0], mods[0], b_sz)
    proj = _inproj_even(h, bf(w_in_e[0]), _rope_tables(s_len, c_len, HD))
    ya = _na_attention(proj, _na_bias_tables(na_rpb[0], s_len // GRID_W), s_len, c_len)
    yw = _wg_attention(proj, wg_sink[0], _wg_mask_tables(s_len), s_len, c_len)
    xs, h32, h16 = _outproj([ya, yw], bf(w_out_e[0]), xs, norm_g[0], mods[0], s_len + c_len, n_lat, b_sz)
    xs, h = _moe_block(xs, h32, h16, norm_g[0, 3], mods[0], n_lat, b_sz, wr_t, b_router, *moe(0),
                       (norm_g[1, 0], mods[1]))

    win = bf(jnp.pad(w_in_o[0], ((0, 0), (0, HD - MLA_ROPE))))
    wqb = bf(jnp.pad(mla_w_qb[0].reshape(MLA_RANK, MLA_HEADS, MLA_NOPE + MLA_ROPE),
                     ((0, 0), (0, 0), (0, MLA_QK - MLA_NOPE - MLA_ROPE))).reshape(MLA_RANK, -1))
    wkvb = mla_w_kvb[0].reshape(MLA_RANK, MLA_HEADS, MLA_NOPE + MLA_V)
    wkn = bf(wkvb[:, :, :MLA_NOPE].reshape(MLA_RANK, -1))
    wv = bf(wkvb[:, :, MLA_NOPE:].reshape(MLA_RANK, -1))
    q_t, k, v_t = _mla_in(h, win, wqb.T, wkn, wv.T, mla_q_norm[0], mla_kv_norm[0],
                          _rope_tables(s_len, c_len, MLA_ROPE))
    y = _mla_attention(q_t, k, v_t, s_len, c_len)
    xl, h32, h16 = _outproj([y], bf(w_out_o[0]), xs, norm_g[1], mods[1], s_len, n_lat, b_sz)
    (out,) = _moe_block(xl, h32, h16, norm_g[1, 3], mods[1], n_lat, b_sz, wr_t, b_router, *moe(1), None)
    return out
```
